```python
import jax, jax.numpy as jnp
from jax import lax
import numpy as np

D_MODEL = 1024
BATCH = 4
SEQ = 4096
DEPTH = 2

N_MIXERS = 2
CONV_WIDTH = 31
HGRN_HEADS = 8
HGRN_KEY_DIM = D_MODEL // HGRN_HEADS
HGRN_VAL_DIM = D_MODEL // HGRN_HEADS
CHUNK = 64
N_GROUPS = 4
EXPERTS_PER_GROUP = 8
N_EXPERTS = N_GROUPS * EXPERTS_PER_GROUP
TOP_K_INNER = 2
D_EXPERT = 512
MOE_BLOCK = 128
NORM_EPS = 1e-6

kernel_name = 'hybrid_conformer_hgrn2_hmoe'


def rms_norm(x, w):
    xf = x.astype(jnp.float32)
    y = xf * lax.rsqrt(jnp.mean(xf * xf, axis=-1, keepdims=True) + NORM_EPS)
    return (y * w.astype(jnp.float32)).astype(x.dtype)


def conformer_conv(h, pw1_w, pw1_b, dw_w, dw_b, ln_g, ln_b, pw2_w, pw2_b):
    a = h @ pw1_w + pw1_b
    u = a[..., :D_MODEL] * jax.nn.sigmoid(a[..., D_MODEL:])
    c = lax.conv_general_dilated(
        u, dw_w[:, None, :].astype(u.dtype), window_strides=(1,),
        padding=[(CONV_WIDTH - 1, 0)], dimension_numbers=('NWC', 'WIO', 'NWC'),
        feature_group_count=D_MODEL) + dw_b
    cf = c.astype(jnp.float32)
    mu = jnp.mean(cf, axis=-1, keepdims=True)
    var = jnp.mean(jnp.square(cf - mu), axis=-1, keepdims=True)
    n = ((cf - mu) * lax.rsqrt(var + NORM_EPS) * ln_g.astype(jnp.float32) + ln_b.astype(jnp.float32)).astype(h.dtype)
    return jax.nn.silu(n) @ pw2_w + pw2_b


def hgrn2(h, w_in, gnorm_w, w_out, lb):
    B, S, _ = h.shape
    nc = S // CHUNK
    proj = h @ w_in
    q, f, i, g = jnp.split(proj, 4, axis=-1)
    q = jax.nn.silu(q.astype(jnp.float32))
    lbf = lb.astype(jnp.float32)
    forget = lbf + (1.0 - lbf) * jax.nn.sigmoid(f.astype(jnp.float32))
    k = 1.0 - forget
    log_f = jnp.log(forget)

    def to_chunks(t, dh):
        return t.astype(jnp.float32).reshape(B, nc, CHUNK, HGRN_HEADS, dh).transpose(1, 0, 3, 2, 4)

    qc = to_chunks(q, HGRN_KEY_DIM)
    kc = to_chunks(k, HGRN_KEY_DIM)
    vc = to_chunks(i, HGRN_VAL_DIM)
    Gc = jnp.cumsum(to_chunks(log_f, HGRN_KEY_DIM), axis=3)
    causal = jnp.tril(jnp.ones((CHUNK, CHUNK), dtype=bool))

    def step(state, inp):
        qb, kb, vb, Gb = inp
        inter = jnp.einsum('bhtk,bhkv->bhtv', qb * jnp.exp(Gb), state)
        diff = Gb[:, :, :, None, :] - Gb[:, :, None, :, :]
        decay = jnp.exp(jnp.where(causal[:, :, None], diff, -jnp.inf))
        scores = jnp.einsum('bhtsk,bhsk->bhts', qb[:, :, :, None, :] * decay, kb)
        intra = jnp.einsum('bhts,bhsv->bhtv', scores, vb)
        G_last = Gb[:, :, -1, :]
        new_state = jnp.exp(G_last)[..., None] * state + jnp.einsum(
            'bhsk,bhsv->bhkv', kb * jnp.exp(G_last[:, :, None, :] - Gb), vb)
        return new_state, inter + intra

    init = jnp.zeros((B, HGRN_HEADS, HGRN_KEY_DIM, HGRN_VAL_DIM), jnp.float32)
    _, o = lax.scan(step, init, (qc, kc, vc, Gc))
    o = o.transpose(1, 0, 3, 2, 4).reshape(B, S, HGRN_HEADS, HGRN_VAL_DIM)
    o = o * lax.rsqrt(jnp.mean(o * o, axis=-1, keepdims=True) + NORM_EPS) * gnorm_w.astype(jnp.float32)
    o = o * jax.nn.silu(g.astype(jnp.float32).reshape(B, S, HGRN_HEADS, HGRN_VAL_DIM))
    return o.reshape(B, S, D_MODEL).astype(h.dtype) @ w_out


def hier_moe(h, grp_w, grp_b, exp_w, exp_b, w_gate, w_up, w_down):
    B, S, D = h.shape
    T = B * S
    hf = h.reshape(T, D)
    grp_prob = jax.nn.softmax((hf @ grp_w + grp_b).astype(jnp.float32), axis=-1)
    grp_val, grp_idx = lax.top_k(grp_prob, 1)
    exp_logits = jnp.einsum('td,dge->tge', hf, exp_w) + exp_b
    in_grp = jnp.take_along_axis(exp_logits, grp_idx[:, :, None], axis=1)[:, 0].astype(jnp.float32)
    top_val, top_idx = lax.top_k(in_grp, TOP_K_INNER)
    gate = (grp_val * jax.nn.softmax(top_val, axis=-1)).reshape(-1)
    expert = (grp_idx * EXPERTS_PER_GROUP + top_idx).reshape(-1)
    token = jnp.repeat(jnp.arange(T, dtype=jnp.int32), TOP_K_INNER)

    onehot = jax.nn.one_hot(expert, N_EXPERTS, dtype=jnp.int32)
    counts = jnp.sum(onehot, axis=0)
    rank = jnp.take_along_axis(jnp.cumsum(onehot, axis=0), expert[:, None], axis=1)[:, 0] - 1
    padded = (counts + MOE_BLOCK - 1) // MOE_BLOCK * MOE_BLOCK
    pad_end = jnp.cumsum(padded)
    pad_start = pad_end - padded
    dest = pad_start[expert] + rank
    n_rows = T * TOP_K_INNER + N_EXPERTS * MOE_BLOCK
    n_blocks = n_rows // MOE_BLOCK
    x_pad = jnp.zeros((n_rows, D), h.dtype).at[dest].set(hf[token])
    tok_pad = jnp.zeros((n_rows,), jnp.int32).at[dest].set(token)
    gate_pad = jnp.zeros((n_rows,), jnp.float32).at[dest].set(gate)
    block_expert = jnp.minimum(
        jnp.searchsorted(pad_end, jnp.arange(n_blocks, dtype=jnp.int32) * MOE_BLOCK, side='right'),
        N_EXPERTS - 1)

    def expert_block(args):
        xb, e = args
        return (jax.nn.silu(xb @ w_gate[e]) * (xb @ w_up[e])) @ w_down[e]

    y = lax.map(expert_block, (x_pad.reshape(n_blocks, MOE_BLOCK, D), block_expert)).reshape(n_rows, D)
    y = (y.astype(jnp.float32) * gate_pad[:, None]).astype(h.dtype)
    return jnp.zeros((T, D), h.dtype).at[tok_pad].add(y).reshape(B, S, D)


def setup_inputs(seed: int = 0) -> dict:
    key = jax.random.key(seed)
    keys = iter(jax.random.split(key, 32))
    n_conv = (DEPTH + 1) // 2
    n_hgrn = DEPTH // 2
    D = D_MODEL

    def nrm(shape, scale):
        return jax.random.normal(next(keys), shape, jnp.float32) * scale

    def gain(shape):
        return 1.0 + nrm(shape, 0.02)

    return {
        'x': nrm((BATCH, SEQ, D), 1.0),
        'conv_norm_w': gain((n_conv, D)),
        'conv_pw1_w': nrm((n_conv, D, 2 * D), D ** -0.5),
        'conv_pw1_b': nrm((n_conv, 2 * D), 0.02),
        'conv_dw_w': nrm((n_conv, CONV_WIDTH, D), CONV_WIDTH ** -0.5),
        'conv_dw_b': nrm((n_conv, D), 0.02),
        'conv_ln_g': gain((n_conv, D)),
        'conv_ln_b': nrm((n_conv, D), 0.02),
        'conv_pw2_w': nrm((n_conv, D, D), D ** -0.5),
        'conv_pw2_b': nrm((n_conv, D), 0.02),
        'hgrn_norm_w': gain((n_hgrn, D)),
        'hgrn_w_in': nrm((n_hgrn, D, 4 * D), D ** -0.5),
        'hgrn_gnorm_w': gain((n_hgrn, HGRN_VAL_DIM)),
        'hgrn_w_out': nrm((n_hgrn, D, D), D ** -0.5),
        'lower_bounds': nrm((DEPTH, D), 0.5),
        'ffn_norm_w': gain((DEPTH, D)),
        'router_grp_w': nrm((DEPTH, D, N_GROUPS), D ** -0.5),
        'router_grp_b': nrm((DEPTH, N_GROUPS), 0.01),
        'router_exp_w': nrm((DEPTH, D, N_GROUPS, EXPERTS_PER_GROUP), D ** -0.5),
        'router_exp_b': nrm((DEPTH, N_GROUPS, EXPERTS_PER_GROUP), 0.01),
        'moe_w_gate': nrm((DEPTH, N_EXPERTS, D, D_EXPERT), D ** -0.5),
        'moe_w_up': nrm((DEPTH, N_EXPERTS, D, D_EXPERT), D ** -0.5),
        'moe_w_down': nrm((DEPTH, N_EXPERTS, D_EXPERT, D), D_EXPERT ** -0.5),
        'final_norm_w': gain((D,)),
    }


def reference(x, conv_norm_w, conv_pw1_w, conv_pw1_b, conv_dw_w, conv_dw_b, conv_ln_g, conv_ln_b,
              conv_pw2_w, conv_pw2_b, hgrn_norm_w, hgrn_w_in, hgrn_gnorm_w, hgrn_w_out, lower_bounds,
              ffn_norm_w, router_grp_w, router_grp_b, router_exp_w, router_exp_b,
              moe_w_gate, moe_w_up, moe_w_down, final_norm_w):
    lb_p = jax.nn.softmax(lower_bounds.astype(jnp.float32), axis=0)
    lb_all = jnp.cumsum(lb_p, axis=0) - lb_p[0]
    h = x
    for layer in range(DEPTH):
        j = layer // N_MIXERS
        if layer % N_MIXERS == 0:
            h = h + conformer_conv(rms_norm(h, conv_norm_w[j]), conv_pw1_w[j], conv_pw1_b[j],
                                   conv_dw_w[j], conv_dw_b[j], conv_ln_g[j], conv_ln_b[j],
                                   conv_pw2_w[j], conv_pw2_b[j])
        else:
            h = h + hgrn2(rms_norm(h, hgrn_norm_w[j]), hgrn_w_in[j], hgrn_gnorm_w[j],
                          hgrn_w_out[j], lb_all[layer])
        h = h + hier_moe(rms_norm(h, ffn_norm_w[layer]), router_grp_w[layer], router_grp_b[layer],
                         router_exp_w[layer], router_exp_b[layer], moe_w_gate[layer],
                         moe_w_up[layer], moe_w_down[layer])
    return rms_norm(h, final_norm_w)
```

```python
import functools

import jax
import jax.numpy as jnp
from jax import lax
from jax.experimental import pallas as pl
from jax.experimental.pallas import tpu as pltpu

F32 = jnp.float32
BF16 = jnp.bfloat16

NORM_EPS = 1e-6
CONV_WIDTH = 31
CONV_HALO = 32
HGRN_HEADS = 8
HEAD_DIM = 128
CHUNK = 64
SUB = 16
N_GROUPS = 4
EXPERTS_PER_GROUP = 8
N_EXPERTS = N_GROUPS * EXPERTS_PER_GROUP
LANES = 128
GROUP_LANE0 = N_EXPERTS

CONV_ROWS = 256
CONV_RC = 64
HGRN_ROWS = 512
ROUTER_ROWS = 512
MOE_BLOCK = 128
COMBINE_ROWS = 256
VMEM_LIMIT = 48 * 1024 * 1024


def _rms(x, w):
    ms = jnp.mean(x * x, axis=-1, keepdims=True)
    return x * lax.rsqrt(ms + NORM_EPS) * w


def _sigmoid(x):
    return 1.0 / (1.0 + jnp.exp(-x))


def _conv_kernel(x_ref, nw_ref, pw1_ref, b1_ref, dw_ref, dwb_ref, lng_ref, lnb_ref,
                 pw2_ref, b2_ref, o_ref, ubuf, cbuf):
    s = pl.program_id(1)
    ts = x_ref.shape[1]
    d = x_ref.shape[2]
    x = x_ref[0]
    hn = _rms(x, nw_ref[...]).astype(BF16)
    a = jnp.dot(hn, pw1_ref[...], preferred_element_type=F32) + b1_ref[...]
    u = a[:, :d] * _sigmoid(a[:, d:])

    @pl.when(s == 0)
    def _():
        ubuf[0:CONV_HALO, :] = jnp.zeros((CONV_HALO, d), F32)

    @pl.when(s > 0)
    def _():
        ubuf[0:CONV_HALO, :] = ubuf[ts:ts + CONV_HALO, :]

    ubuf[CONV_HALO:, :] = u

    off0 = CONV_HALO - (CONV_WIDTH - 1)
    for rc in range(ts // CONV_RC):
        for lc in range(d // LANES):
            ls = slice(lc * LANES, (lc + 1) * LANES)
            acc = jnp.broadcast_to(dwb_ref[:, ls], (CONV_RC, LANES))
            for j in range(CONV_WIDTH):
                r = rc * CONV_RC + off0 + j
                acc = acc + dw_ref[j:j + 1, ls] * ubuf[r:r + CONV_RC, ls]
            cbuf[rc * CONV_RC:(rc + 1) * CONV_RC, ls] = acc

    c = cbuf[...]
    mu = jnp.mean(c, axis=-1, keepdims=True)
    cc = c - mu
    var = jnp.mean(cc * cc, axis=-1, keepdims=True)
    n = cc * lax.rsqrt(var + NORM_EPS) * lng_ref[...] + lnb_ref[...]
    sw = (n * _sigmoid(n)).astype(BF16)
    y = jnp.dot(sw, pw2_ref[...], preferred_element_type=F32) + b2_ref[...]
    o_ref[0] = x + y


def _conformer_layer(h, nw, pw1, b1, dw, dwb, lng, lnb, pw2, b2):
    b, s, d = h.shape
    ts = CONV_ROWS
    row = lambda v: v.reshape(1, -1)
    full = lambda shape: pl.BlockSpec(shape, lambda bi, si: (0,) * len(shape))
    return pl.pallas_call(
        _conv_kernel,
        grid=(b, s // ts),
        in_specs=[
            pl.BlockSpec((1, ts, d), lambda bi, si: (bi, si, 0)),
            full((1, d)), full((d, 2 * d)), full((1, 2 * d)),
            full((CONV_WIDTH, d)), full((1, d)), full((1, d)), full((1, d)),
            full((d, d)), full((1, d)),
        ],
        out_specs=pl.BlockSpec((1, ts, d), lambda bi, si: (bi, si, 0)),
        out_shape=jax.ShapeDtypeStruct((b, s, d), F32),
        scratch_shapes=[pltpu.VMEM((ts + CONV_HALO, d), F32), pltpu.VMEM((ts, d), F32)],
        compiler_params=pltpu.CompilerParams(
            dimension_semantics=("arbitrary", "arbitrary"), vmem_limit_bytes=VMEM_LIMIT),
        name="conformer_conv",
    )(h, row(nw), pw1.astype(BF16), row(b1), dw, row(dwb), row(lng), row(lnb),
      pw2.astype(BF16), row(b2))


def _hgrn_chunk(c, h, q_s, k_s, v_s, g_s, o_s, st_ref):
    r0 = pl.multiple_of(c * CHUNK, CHUNK)
    rows = pl.ds(r0, CHUNK)
    q = q_s[rows, :]
    k = k_s[rows, :]
    v = v_s[rows, :]
    gc = g_s[rows, :]
    vb = v.astype(BF16)
    st = st_ref[h]
    nt = (((1,), (1,)), ((), ()))

    qg = (q * jnp.exp(gc)).astype(BF16)
    inter = lax.dot_general(qg, st.astype(BF16), nt, preferred_element_type=F32)

    row = lax.broadcasted_iota(jnp.int32, (CHUNK, 1), 0)
    neg = jnp.float32(-jnp.inf)
    nsub = CHUNK // SUB

    blocks = [jnp.zeros((SUB, CHUNK), F32)]
    for i in range(1, nsub):
        gref = gc[i * SUB - 1:i * SUB, :]
        qs = q[i * SUB:(i + 1) * SUB, :] * jnp.exp(gc[i * SUB:(i + 1) * SUB, :] - gref)
        ks = k * jnp.exp(jnp.where(row < i * SUB, gref - gc, neg))
        blocks.append(lax.dot_general(qs.astype(BF16), ks.astype(BF16), nt,
                                      preferred_element_type=F32))
    scores = jnp.concatenate(blocks, axis=0)

    tloc = row % SUB
    base = row - tloc
    lane = lax.broadcasted_iota(jnp.int32, (CHUNK, CHUNK), 1)
    for s in range(SUB):
        gs = jnp.concatenate(
            [jnp.broadcast_to(gc[i * SUB + s:i * SUB + s + 1, :], (SUB, HEAD_DIM))
             for i in range(nsub)], axis=0)
        ks = jnp.concatenate(
            [jnp.broadcast_to(k[i * SUB + s:i * SUB + s + 1, :], (SUB, HEAD_DIM))
             for i in range(nsub)], axis=0)
        e = jnp.exp(jnp.where(tloc >= s, gc - gs, neg))
        col = jnp.sum(q * e * ks, axis=-1, keepdims=True)
        scores = jnp.where(lane == base + s, col, scores)

    intra = jnp.dot(scores.astype(BF16), vb, preferred_element_type=F32)
    o_s[rows, :] = inter + intra

    glast = gc[CHUNK - 1:CHUNK, :]
    kd = (k * jnp.exp(glast - gc)).astype(BF16)
    upd = lax.dot_general(vb, kd, (((0,), (0,)), ((), ())), preferred_element_type=F32)
    st_ref[h] = jnp.exp(glast) * st + upd


def _hgrn_kernel(x_ref, nw_ref, win_ref, lb_ref, gn_ref, wout_ref, o_ref,
                 hn_s, q_s, k_s, v_s, g_s, z_s, o_s, st_ref):
    s = pl.program_id(1)
    h = pl.program_id(2)
    ts = x_ref.shape[1]

    @pl.when(h == 0)
    def _():
        hn_s[...] = _rms(x_ref[0], nw_ref[...]).astype(BF16)

    @pl.when(s == 0)
    def _():
        st_ref[h] = jnp.zeros((HEAD_DIM, HEAD_DIM), F32)

    proj = jnp.dot(hn_s[...], win_ref[0], preferred_element_type=F32)
    qr = proj[:, 0:HEAD_DIM]
    q_s[...] = qr * _sigmoid(qr)
    lb = lb_ref[0]
    forget = lb + (1.0 - lb) * _sigmoid(proj[:, HEAD_DIM:2 * HEAD_DIM])
    k_s[...] = 1.0 - forget
    v_s[...] = proj[:, 2 * HEAD_DIM:3 * HEAD_DIM]
    zr = proj[:, 3 * HEAD_DIM:4 * HEAD_DIM]
    z_s[...] = zr * _sigmoid(zr)

    g = jnp.log(forget)
    rmod = lax.broadcasted_iota(jnp.int32, (ts, 1), 0) % CHUNK
    dstep = 1
    while dstep < CHUNK:
        g = g + jnp.where(rmod >= dstep, pltpu.roll(g, dstep, axis=0), 0.0)
        dstep *= 2
    g_s[...] = g

    def body(c, carry):
        _hgrn_chunk(c, h, q_s, k_s, v_s, g_s, o_s, st_ref)
        return carry

    lax.fori_loop(0, ts // CHUNK, body, 0)

    o = o_s[...]
    o = o * lax.rsqrt(jnp.mean(o * o, axis=-1, keepdims=True) + NORM_EPS) * gn_ref[...]
    o = (o * z_s[...]).astype(BF16)
    contrib = jnp.dot(o, wout_ref[...], preferred_element_type=F32)

    @pl.when(h == 0)
    def _():
        o_ref[0] = x_ref[0] + contrib

    @pl.when(h > 0)
    def _():
        o_ref[0] = o_ref[0] + contrib


def _hgrn_layer(h, nw, w_in, gn, w_out, lb):
    b, s, d = h.shape
    ts = HGRN_ROWS
    nh = HGRN_HEADS
    win_r = w_in.reshape(d, 4, nh, HEAD_DIM).transpose(2, 0, 1, 3).reshape(nh, d, 4 * HEAD_DIM)
    return pl.pallas_call(
        _hgrn_kernel,
        grid=(b, s // ts, nh),
        in_specs=[
            pl.BlockSpec((1, ts, d), lambda bi, si, hi: (bi, si, 0)),
            pl.BlockSpec((1, d), lambda bi, si, hi: (0, 0)),
            pl.BlockSpec((1, d, 4 * HEAD_DIM), lambda bi, si, hi: (hi, 0, 0)),
            pl.BlockSpec((1, 1, HEAD_DIM), lambda bi, si, hi: (hi, 0, 0)),
            pl.BlockSpec((1, HEAD_DIM), lambda bi, si, hi: (0, 0)),
            pl.BlockSpec((HEAD_DIM, d), lambda bi, si, hi: (hi, 0)),
        ],
        out_specs=pl.BlockSpec((1, ts, d), lambda bi, si, hi: (bi, si, 0)),
        out_shape=jax.ShapeDtypeStruct((b, s, d), F32),
        scratch_shapes=[
            pltpu.VMEM((ts, d), BF16),
            pltpu.VMEM((ts, HEAD_DIM), F32), pltpu.VMEM((ts, HEAD_DIM), F32),
            pltpu.VMEM((ts, HEAD_DIM), F32), pltpu.VMEM((ts, HEAD_DIM), F32),
            pltpu.VMEM((ts, HEAD_DIM), F32), pltpu.VMEM((ts, HEAD_DIM), F32),
            pltpu.VMEM((nh, HEAD_DIM, HEAD_DIM), F32),
        ],
        compiler_params=pltpu.CompilerParams(
            dimension_semantics=("arbitrary", "arbitrary", "arbitrary"),
            vmem_limit_bytes=VMEM_LIMIT),
        name="hgrn2",
    )(h, nw.reshape(1, d), win_r.astype(BF16), lb.reshape(nh, 1, HEAD_DIM),
      gn.reshape(1, HEAD_DIM), w_out.astype(BF16))


def _router_kernel(h_ref, nw_ref, wr_ref, br_ref, idx_ref, gate_ref, cnt_ref, cnt_s):
    i = pl.program_id(0)
    tm = h_ref.shape[0]

    @pl.when(i == 0)
    def _():
        cnt_s[...] = jnp.zeros_like(cnt_s)

    hn = _rms(h_ref[...], nw_ref[...])
    logits = jnp.dot(hn, wr_ref[...], preferred_element_type=F32,
                     precision=lax.Precision.HIGHEST) + br_ref[...]
    li = lax.broadcasted_iota(jnp.int32, (tm, LANES), 1).astype(F32)
    neg = jnp.float32(-jnp.inf)
    big = jnp.float32(LANES)

    gl = jnp.where((li >= GROUP_LANE0) & (li < GROUP_LANE0 + N_GROUPS), logits, neg)
    gmax = jnp.max(gl, axis=-1, keepdims=True)
    gval = 1.0 / jnp.sum(jnp.exp(gl - gmax), axis=-1, keepdims=True)
    gidx = jnp.min(jnp.where(gl == gmax, li, big), axis=-1, keepdims=True) - GROUP_LANE0

    lo = gidx * EXPERTS_PER_GROUP
    el = jnp.where((li >= lo) & (li < lo + EXPERTS_PER_GROUP), logits, neg)
    v0 = jnp.max(el, axis=-1, keepdims=True)
    i0 = jnp.min(jnp.where(el == v0, li, big), axis=-1, keepdims=True)
    el2 = jnp.where(li == i0, neg, el)
    v1 = jnp.max(el2, axis=-1, keepdims=True)
    i1 = jnp.min(jnp.where(el2 == v1, li, big), axis=-1, keepdims=True)
    t = jnp.exp(v1 - v0)
    p0 = 1.0 / (1.0 + t)
    g0 = gval * p0
    g1 = gval * (t * p0)

    oh = jnp.where((li == i0) | (li == i1), 1.0, 0.0)
    rr = lax.broadcasted_iota(jnp.int32, (tm, tm), 0)
    cc = lax.broadcasted_iota(jnp.int32, (tm, tm), 1)
    lower = jnp.where(rr > cc, 1.0, 0.0).astype(BF16)
    before = jnp.dot(lower, oh.astype(BF16), preferred_element_type=F32) + cnt_s[...]
    r0 = jnp.sum(jnp.where(li == i0, before, 0.0), axis=-1, keepdims=True)
    r1 = jnp.sum(jnp.where(li == i1, before, 0.0), axis=-1, keepdims=True)
    cnt_s[...] = cnt_s[...] + jnp.sum(oh, axis=0, keepdims=True)

    rec = jnp.where(li == 0.0, i0, jnp.where(li == 1.0, i1,
                    jnp.where(li == 2.0, r0, jnp.where(li == 3.0, r1, 0.0))))
    idx_ref[...] = rec.astype(jnp.int32)
    gate_ref[...] = jnp.where(li == 0.0, g0, jnp.where(li == 1.0, g1, 0.0))
    cnt_ref[...] = cnt_s[...]


def _router(hf, nw, grp_w, grp_b, exp_w, exp_b):
    t, d = hf.shape
    tm = ROUTER_ROWS
    wr = jnp.zeros((d, LANES), F32)
    wr = wr.at[:, :N_EXPERTS].set(exp_w.reshape(d, N_EXPERTS))
    wr = wr.at[:, GROUP_LANE0:GROUP_LANE0 + N_GROUPS].set(grp_w)
    br = jnp.zeros((1, LANES), F32)
    br = br.at[0, :N_EXPERTS].set(exp_b.reshape(N_EXPERTS))
    br = br.at[0, GROUP_LANE0:GROUP_LANE0 + N_GROUPS].set(grp_b)
    return pl.pallas_call(
        _router_kernel,
        grid=(t // tm,),
        in_specs=[
            pl.BlockSpec((tm, d), lambda i: (i, 0)),
            pl.BlockSpec((1, d), lambda i: (0, 0)),
            pl.BlockSpec((d, LANES), lambda i: (0, 0)),
            pl.BlockSpec((1, LANES), lambda i: (0, 0)),
        ],
        out_specs=[
            pl.BlockSpec((tm, LANES), lambda i: (i, 0)),
            pl.BlockSpec((tm, LANES), lambda i: (i, 0)),
            pl.BlockSpec((1, LANES), lambda i: (0, 0)),
        ],
        out_shape=[
            jax.ShapeDtypeStruct((t, LANES), jnp.int32),
            jax.ShapeDtypeStruct((t, LANES), F32),
            jax.ShapeDtypeStruct((1, LANES), F32),
        ],
        scratch_shapes=[pltpu.VMEM((1, LANES), F32)],
        compiler_params=pltpu.CompilerParams(
            dimension_semantics=("arbitrary",), vmem_limit_bytes=VMEM_LIMIT),
        name="moe_router",
    )(hf, nw.reshape(1, d), wr, br)


def _expert_kernel(be_ref, tok_ref, h_hbm, nw_ref, wg_ref, wu_ref, wd_ref, y_ref, xbuf, sem):
    i = pl.program_id(0)
    nb = pl.num_programs(0)
    blk = xbuf.shape[1]

    def row_copy(b, r, slot):
        tok = tok_ref[b * blk + r]
        return pltpu.make_async_copy(h_hbm.at[pl.ds(tok, 1)], xbuf.at[slot, pl.ds(r, 1)],
                                     sem.at[slot])

    def issue(b, slot):
        def body(r, carry):
            row_copy(b, r, slot).start()
            return carry
        lax.fori_loop(0, blk, body, 0)

    @pl.when(i == 0)
    def _():
        issue(0, 0)

    @pl.when(i + 1 < nb)
    def _():
        issue(i + 1, (i + 1) % 2)

    slot = i % 2
    pltpu.make_async_copy(h_hbm.at[pl.ds(0, blk)], xbuf.at[slot], sem.at[slot]).wait()

    hn = _rms(xbuf[slot], nw_ref[...]).astype(BF16)
    g = jnp.dot(hn, wg_ref[0], preferred_element_type=F32)
    u = jnp.dot(hn, wu_ref[0], preferred_element_type=F32)
    a = (g * _sigmoid(g) * u).astype(BF16)
    y_ref[...] = jnp.dot(a, wd_ref[0], preferred_element_type=F32)


def _experts(hf, nw, block_expert, tok_row, w_gate, w_up, w_down):
    t, d = hf.shape
    n_rows = tok_row.shape[0]
    blk = MOE_BLOCK
    de = w_gate.shape[-1]
    grid_spec = pltpu.PrefetchScalarGridSpec(
        num_scalar_prefetch=2,
        grid=(n_rows // blk,),
        in_specs=[
            pl.BlockSpec(memory_space=pl.ANY),
            pl.BlockSpec((1, d), lambda i, be, tr: (0, 0)),
            pl.BlockSpec((1, d, de), lambda i, be, tr: (be[i], 0, 0)),
            pl.BlockSpec((1, d, de), lambda i, be, tr: (be[i], 0, 0)),
            pl.BlockSpec((1, de, d), lambda i, be, tr: (be[i], 0, 0)),
        ],
        out_specs=pl.BlockSpec((blk, d), lambda i, be, tr: (i, 0)),
        scratch_shapes=[pltpu.VMEM((2, blk, d), F32), pltpu.SemaphoreType.DMA((2,))],
    )
    return pl.pallas_call(
        _expert_kernel,
        grid_spec=grid_spec,
        out_shape=jax.ShapeDtypeStruct((n_rows, d), F32),
        compiler_params=pltpu.CompilerParams(
            dimension_semantics=("arbitrary",), vmem_limit_bytes=VMEM_LIMIT),
        name="moe_experts",
    )(block_expert, tok_row, hf, nw.reshape(1, d), w_gate.astype(BF16), w_up.astype(BF16),
      w_down.astype(BF16))


def _combine_kernel(d0_ref, d1_ref, y_hbm, h_ref, gate_ref, fw_ref, o_ref, ybuf, sem, *, final_norm):
    i = pl.program_id(0)
    nb = pl.num_programs(0)
    tc = h_ref.shape[0]

    def issue(b, slot):
        def body(r, carry):
            pltpu.make_async_copy(y_hbm.at[pl.ds(d0_ref[b * tc + r], 1)],
                                  ybuf.at[slot, 0, pl.ds(r, 1)], sem.at[slot]).start()
            pltpu.make_async_copy(y_hbm.at[pl.ds(d1_ref[b * tc + r], 1)],
                                  ybuf.at[slot, 1, pl.ds(r, 1)], sem.at[slot]).start()
            return carry
        lax.fori_loop(0, tc, body, 0)

    @pl.when(i == 0)
    def _():
        issue(0, 0)

    @pl.when(i + 1 < nb)
    def _():
        issue(i + 1, (i + 1) % 2)

    slot = i % 2
    for kk in range(2):
        pltpu.make_async_copy(y_hbm.at[pl.ds(0, tc)], ybuf.at[slot, kk], sem.at[slot]).wait()

    gate = gate_ref[...]
    o = h_ref[...] + (gate[:, 0:1] * ybuf[slot, 0] + gate[:, 1:2] * ybuf[slot, 1])
    if final_norm:
        o = _rms(o, fw_ref[...])
    o_ref[...] = o


def _combine(hf, y, dest0, dest1, gates, fw, final_norm):
    t, d = hf.shape
    tc = COMBINE_ROWS
    grid_spec = pltpu.PrefetchScalarGridSpec(
        num_scalar_prefetch=2,
        grid=(t // tc,),
        in_specs=[
            pl.BlockSpec(memory_space=pl.ANY),
            pl.BlockSpec((tc, d), lambda i, a, b: (i, 0)),
            pl.BlockSpec((tc, LANES), lambda i, a, b: (i, 0)),
            pl.BlockSpec((1, d), lambda i, a, b: (0, 0)),
        ],
        out_specs=pl.BlockSpec((tc, d), lambda i, a, b: (i, 0)),
        scratch_shapes=[pltpu.VMEM((2, 2, tc, d), F32), pltpu.SemaphoreType.DMA((2,))],
    )
    return pl.pallas_call(
        functools.partial(_combine_kernel, final_norm=final_norm),
        grid_spec=grid_spec,
        out_shape=jax.ShapeDtypeStruct((t, d), F32),
        compiler_params=pltpu.CompilerParams(
            dimension_semantics=("arbitrary",), vmem_limit_bytes=VMEM_LIMIT),
        name="moe_combine",
    )(dest0, dest1, y, hf, gates, fw.reshape(1, d))


def _moe_layer(h, nw, grp_w, grp_b, exp_w, exp_b, w_gate, w_up, w_down, fw, final_norm):
    b, s, d = h.shape
    t = b * s
    hf = h.reshape(t, d)
    idx, gates, cnt = _router(hf, nw, grp_w, grp_b, exp_w, exp_b)

    e0, e1, r0, r1 = idx[:, 0], idx[:, 1], idx[:, 2], idx[:, 3]
    counts = cnt[0, :N_EXPERTS].astype(jnp.int32)
    padded = (counts + MOE_BLOCK - 1) // MOE_BLOCK * MOE_BLOCK
    pad_end = jnp.cumsum(padded)
    pad_start = pad_end - padded
    dest0 = pad_start[e0] + r0
    dest1 = pad_start[e1] + r1
    n_rows = 2 * t + N_EXPERTS * MOE_BLOCK
    n_blocks = n_rows // MOE_BLOCK
    tok = jnp.arange(t, dtype=jnp.int32)
    tok_row = jnp.zeros((n_rows,), jnp.int32).at[dest0].set(tok).at[dest1].set(tok)
    block_expert = jnp.minimum(
        jnp.searchsorted(pad_end, jnp.arange(n_blocks, dtype=jnp.int32) * MOE_BLOCK, side='right'),
        N_EXPERTS - 1).astype(jnp.int32)

    y = _experts(hf, nw, block_expert, tok_row, w_gate, w_up, w_down)
    out = _combine(hf, y, dest0, dest1, gates, fw, final_norm)
    return out.reshape(b, s, d)


def kernel(x, conv_norm_w, conv_pw1_w, conv_pw1_b, conv_dw_w, conv_dw_b, conv_ln_g, conv_ln_b,
           conv_pw2_w, conv_pw2_b, hgrn_norm_w, hgrn_w_in, hgrn_gnorm_w, hgrn_w_out, lower_bounds,
           ffn_norm_w, router_grp_w, router_grp_b, router_exp_w, router_exp_b,
           moe_w_gate, moe_w_up, moe_w_down, final_norm_w):
    depth = lower_bounds.shape[0]
    lb_p = jax.nn.softmax(lower_bounds.astype(F32), axis=0)
    lb_all = jnp.cumsum(lb_p, axis=0) - lb_p[0]
    h = x
    for layer in range(depth):
        j = layer // 2
        if layer % 2 == 0:
            h = _conformer_layer(h, conv_norm_w[j], conv_pw1_w[j], conv_pw1_b[j], conv_dw_w[j],
                                 conv_dw_b[j], conv_ln_g[j], conv_ln_b[j], conv_pw2_w[j],
                                 conv_pw2_b[j])
        else:
            h = _hgrn_layer(h, hgrn_norm_w[j], hgrn_w_in[j], hgrn_gnorm_w[j], hgrn_w_out[j],
                            lb_all[layer])
        h = _moe_layer(h, ffn_norm_w[layer], router_grp_w[layer], router_grp_b[layer],
                       router_exp_w[layer], router_exp_b[layer], moe_w_gate[layer],
                       moe_w_up[layer], moe_w_down[layer], final_norm_w,
                       final_norm=(layer == depth - 1))
    return h
```

```python
import functools

import jax
import jax.numpy as jnp
from jax import lax
from jax.experimental import pallas as pl
from jax.experimental.pallas import tpu as pltpu

F32 = jnp.float32
BF16 = jnp.bfloat16

NORM_EPS = 1e-6
CONV_WIDTH = 31
CONV_HALO = 32
HGRN_HEADS = 8
HEAD_DIM = 128
CHUNK = 64
N_GROUPS = 4
EXPERTS_PER_GROUP = 8
N_EXPERTS = N_GROUPS * EXPERTS_PER_GROUP
LANES = 128
SUBLANES = 8
GROUP_LANE0 = N_EXPERTS

CONV_ROWS = 256
CONV_RC = 64
HGRN_ROWS = 512
ROUTER_ROWS = 512
MOE_BLOCK = 128
DMA_UNROLL = 4
DISPATCH_ROWS = 256
COMBINE_ROWS = 256
VMEM_LIMIT = 48 * 1024 * 1024


def _rms(x, w):
    ms = jnp.mean(x * x, axis=-1, keepdims=True)
    return x * lax.rsqrt(ms + NORM_EPS) * w


def _sigmoid(x):
    return 1.0 / (1.0 + jnp.exp(-x))


def _conv_kernel(x_ref, nw_ref, pw1_ref, b1_ref, dw_ref, dwb_ref, lng_ref, lnb_ref,
                 pw2_ref, b2_ref, o_ref, ubuf, cbuf):
    s = pl.program_id(1)
    ts = x_ref.shape[1]
    d = x_ref.shape[2]
    x = x_ref[0]
    hn = _rms(x, nw_ref[...]).astype(BF16)
    a = jnp.dot(hn, pw1_ref[...], preferred_element_type=F32) + b1_ref[...]
    u = a[:, :d] * _sigmoid(a[:, d:])

    @pl.when(s == 0)
    def _():
        ubuf[0:CONV_HALO, :] = jnp.zeros((CONV_HALO, d), F32)

    @pl.when(s > 0)
    def _():
        ubuf[0:CONV_HALO, :] = ubuf[ts:ts + CONV_HALO, :]

    ubuf[CONV_HALO:, :] = u

    off0 = CONV_HALO - (CONV_WIDTH - 1)
    win = CONV_RC + CONV_HALO
    for rc in range(ts // CONV_RC):
        for lc in range(d // LANES):
            ls = slice(lc * LANES, (lc + 1) * LANES)
            xw = ubuf[rc * CONV_RC:rc * CONV_RC + win, ls]
            acc = jnp.broadcast_to(dwb_ref[:, ls], (CONV_RC, LANES))
            for p in range(SUBLANES):
                xp = xw if p == 0 else pltpu.roll(xw, win - p, axis=0)
                for j in range(CONV_WIDTH):
                    if (off0 + j) % SUBLANES == p:
                        a0 = off0 + j - p
                        acc = acc + dw_ref[j:j + 1, ls] * xp[a0:a0 + CONV_RC, :]
            cbuf[rc * CONV_RC:(rc + 1) * CONV_RC, ls] = acc

    c = cbuf[...]
    mu = jnp.mean(c, axis=-1, keepdims=True)
    cc = c - mu
    var = jnp.mean(cc * cc, axis=-1, keepdims=True)
    n = cc * lax.rsqrt(var + NORM_EPS) * lng_ref[...] + lnb_ref[...]
    sw = (n * _sigmoid(n)).astype(BF16)
    y = jnp.dot(sw, pw2_ref[...], preferred_element_type=F32) + b2_ref[...]
    o_ref[0] = x + y


def _conformer_layer(h, nw, pw1, b1, dw, dwb, lng, lnb, pw2, b2):
    b, s, d = h.shape
    ts = CONV_ROWS
    row = lambda v: v.reshape(1, -1)
    full = lambda shape: pl.BlockSpec(shape, lambda bi, si: (0,) * len(shape))
    return pl.pallas_call(
        _conv_kernel,
        grid=(b, s // ts),
        in_specs=[
            pl.BlockSpec((1, ts, d), lambda bi, si: (bi, si, 0)),
            full((1, d)), full((d, 2 * d)), full((1, 2 * d)),
            full((CONV_WIDTH, d)), full((1, d)), full((1, d)), full((1, d)),
            full((d, d)), full((1, d)),
        ],
        out_specs=pl.BlockSpec((1, ts, d), lambda bi, si: (bi, si, 0)),
        out_shape=jax.ShapeDtypeStruct((b, s, d), F32),
        scratch_shapes=[pltpu.VMEM((ts + CONV_HALO, d), F32), pltpu.VMEM((ts, d), F32)],
        compiler_params=pltpu.CompilerParams(
            dimension_semantics=("arbitrary", "arbitrary"), vmem_limit_bytes=VMEM_LIMIT),
        name="conformer_conv",
    )(h, row(nw), pw1.astype(BF16), row(b1), dw, row(dwb), row(lng), row(lnb),
      pw2.astype(BF16), row(b2))


def _hgrn_chunk(c, h, q_s, k_s, v_s, g_s, o_s, st_ref):
    rows = pl.ds(c * CHUNK, CHUNK)
    q = q_s[rows, :]
    k = k_s[rows, :]
    v = v_s[rows, :]
    gc = g_s[rows, :]
    vb = v.astype(BF16)
    st = st_ref[h]
    nt = (((1,), (1,)), ((), ()))

    qg = (q * jnp.exp(gc)).astype(BF16)
    inter = lax.dot_general(qg, st.astype(BF16), nt, preferred_element_type=F32)

    row = lax.broadcasted_iota(jnp.int32, (CHUNK, HEAD_DIM), 0)
    neg = jnp.float32(-jnp.inf)
    tx = (lax.broadcasted_iota(jnp.int32, (CHUNK, CHUNK), 0)
          ^ lax.broadcasted_iota(jnp.int32, (CHUNK, CHUNK), 1))
    diag = jnp.sum(q * k, axis=-1, keepdims=True)
    scores = jnp.where(tx == 0, diag, 0.0)
    end = gc
    b = 1
    while b < CHUNK:
        upper = (row & b) != 0
        qs = q * jnp.exp(jnp.where(upper, gc - pltpu.roll(end, b, axis=0), neg))
        ks = k * jnp.exp(jnp.where(upper, neg, end - gc))
        sb = lax.dot_general(qs.astype(BF16), ks.astype(BF16), nt, preferred_element_type=F32)
        scores = jnp.where(tx >= b, sb, scores)
        end = jnp.where(upper, end, pltpu.roll(end, CHUNK - b, axis=0))
        b *= 2

    intra = jnp.dot(scores.astype(BF16), vb, preferred_element_type=F32)
    o_s[rows, :] = inter + intra

    glast = gc[CHUNK - 1:CHUNK, :]
    kd = (k * jnp.exp(glast - gc)).astype(BF16)
    upd = lax.dot_general(vb, kd, (((0,), (0,)), ((), ())), preferred_element_type=F32)
    st_ref[h] = jnp.exp(glast) * st + upd


def _hgrn_kernel(x_ref, nw_ref, win_ref, lb_ref, gn_ref, wout_ref, o_ref,
                 hn_s, q_s, k_s, v_s, g_s, z_s, o_s, oh_s, st_ref):
    s = pl.program_id(1)
    h = pl.program_id(2)
    ts = x_ref.shape[1]

    @pl.when(h == 0)
    def _():
        hn_s[...] = _rms(x_ref[0], nw_ref[...]).astype(BF16)

    @pl.when(s == 0)
    def _():
        st_ref[h] = jnp.zeros((HEAD_DIM, HEAD_DIM), F32)

    proj = jnp.dot(hn_s[...], win_ref[0], preferred_element_type=F32)
    qr = proj[:, 0:HEAD_DIM]
    q_s[...] = qr * _sigmoid(qr)
    lb = lb_ref[0]
    forget = lb + (1.0 - lb) * _sigmoid(proj[:, HEAD_DIM:2 * HEAD_DIM])
    k_s[...] = 1.0 - forget
    v_s[...] = proj[:, 2 * HEAD_DIM:3 * HEAD_DIM]
    zr = proj[:, 3 * HEAD_DIM:4 * HEAD_DIM]
    z_s[...] = zr * _sigmoid(zr)

    g = jnp.log(forget)
    rmod = lax.broadcasted_iota(jnp.int32, (ts, 1), 0) % CHUNK
    dstep = 1
    while dstep < CHUNK:
        g = g + jnp.where(rmod >= dstep, pltpu.roll(g, dstep, axis=0), 0.0)
        dstep *= 2
    g_s[...] = g

    for c in range(ts // CHUNK):
        _hgrn_chunk(c, h, q_s, k_s, v_s, g_s, o_s, st_ref)

    o = o_s[...]
    o = o * lax.rsqrt(jnp.mean(o * o, axis=-1, keepdims=True) + NORM_EPS) * gn_ref[...]
    oh_s[h] = (o * z_s[...]).astype(BF16)

    @pl.when(h == pl.num_programs(2) - 1)
    def _():
        oall = jnp.concatenate([oh_s[hh] for hh in range(HGRN_HEADS)], axis=1)
        o_ref[0] = x_ref[0] + jnp.dot(oall, wout_ref[...], preferred_element_type=F32)


def _hgrn_layer(h, nw, w_in, gn, w_out, lb):
    b, s, d = h.shape
    ts = HGRN_ROWS
    nh = HGRN_HEADS
    win_r = w_in.reshape(d, 4, nh, HEAD_DIM).transpose(2, 0, 1, 3).reshape(nh, d, 4 * HEAD_DIM)
    return pl.pallas_call(
        _hgrn_kernel,
        grid=(b, s // ts, nh),
        in_specs=[
            pl.BlockSpec((1, ts, d), lambda bi, si, hi: (bi, si, 0)),
            pl.BlockSpec((1, d), lambda bi, si, hi: (0, 0)),
            pl.BlockSpec((1, d, 4 * HEAD_DIM), lambda bi, si, hi: (hi, 0, 0)),
            pl.BlockSpec((1, 1, HEAD_DIM), lambda bi, si, hi: (hi, 0, 0)),
            pl.BlockSpec((1, HEAD_DIM), lambda bi, si, hi: (0, 0)),
            pl.BlockSpec((d, d), lambda bi, si, hi: (0, 0)),
        ],
        out_specs=pl.BlockSpec((1, ts, d), lambda bi, si, hi: (bi, si, 0)),
        out_shape=jax.ShapeDtypeStruct((b, s, d), F32),
        scratch_shapes=[
            pltpu.VMEM((ts, d), BF16),
            pltpu.VMEM((ts, HEAD_DIM), F32), pltpu.VMEM((ts, HEAD_DIM), F32),
            pltpu.VMEM((ts, HEAD_DIM), F32), pltpu.VMEM((ts, HEAD_DIM), F32),
            pltpu.VMEM((ts, HEAD_DIM), F32), pltpu.VMEM((ts, HEAD_DIM), F32),
            pltpu.VMEM((nh, ts, HEAD_DIM), BF16),
            pltpu.VMEM((nh, HEAD_DIM, HEAD_DIM), F32),
        ],
        compiler_params=pltpu.CompilerParams(
            dimension_semantics=("arbitrary", "arbitrary", "arbitrary"),
            vmem_limit_bytes=VMEM_LIMIT),
        name="hgrn2",
    )(h, nw.reshape(1, d), win_r.astype(BF16), lb.reshape(nh, 1, HEAD_DIM),
      gn.reshape(1, HEAD_DIM), w_out.astype(BF16))


def _router_kernel(h_ref, nw_ref, wr_ref, br_ref, idx_ref, gate_ref, cnt_ref, cnt_s):
    i = pl.program_id(0)
    tm = h_ref.shape[0]

    @pl.when(i == 0)
    def _():
        cnt_s[...] = jnp.zeros_like(cnt_s)

    hn = _rms(h_ref[...], nw_ref[...])
    logits = jnp.dot(hn, wr_ref[...], preferred_element_type=F32,
                     precision=lax.Precision.HIGHEST) + br_ref[...]
    li = lax.broadcasted_iota(jnp.int32, (tm, LANES), 1).astype(F32)
    neg = jnp.float32(-jnp.inf)
    big = jnp.float32(LANES)

    gl = jnp.where((li >= GROUP_LANE0) & (li < GROUP_LANE0 + N_GROUPS), logits, neg)
    gmax = jnp.max(gl, axis=-1, keepdims=True)
    gval = 1.0 / jnp.sum(jnp.exp(gl - gmax), axis=-1, keepdims=True)
    gidx = jnp.min(jnp.where(gl == gmax, li, big), axis=-1, keepdims=True) - GROUP_LANE0

    lo = gidx * EXPERTS_PER_GROUP
    el = jnp.where((li >= lo) & (li < lo + EXPERTS_PER_GROUP), logits, neg)
    v0 = jnp.max(el, axis=-1, keepdims=True)
    i0 = jnp.min(jnp.where(el == v0, li, big), axis=-1, keepdims=True)
    el2 = jnp.where(li == i0, neg, el)
    v1 = jnp.max(el2, axis=-1, keepdims=True)
    i1 = jnp.min(jnp.where(el2 == v1, li, big), axis=-1, keepdims=True)
    t = jnp.exp(v1 - v0)
    p0 = 1.0 / (1.0 + t)
    g0 = gval * p0
    g1 = gval * (t * p0)

    oh = jnp.where((li == i0) | (li == i1), 1.0, 0.0)
    rr = lax.broadcasted_iota(jnp.int32, (tm, tm), 0)
    cc = lax.broadcasted_iota(jnp.int32, (tm, tm), 1)
    lower = jnp.where(rr > cc, 1.0, 0.0).astype(BF16)
    before = jnp.dot(lower, oh.astype(BF16), preferred_element_type=F32) + cnt_s[...]
    r0 = jnp.sum(jnp.where(li == i0, before, 0.0), axis=-1, keepdims=True)
    r1 = jnp.sum(jnp.where(li == i1, before, 0.0), axis=-1, keepdims=True)
    cnt_s[...] = cnt_s[...] + jnp.sum(oh, axis=0, keepdims=True)

    rec = jnp.where(li == 0.0, i0, jnp.where(li == 1.0, i1,
                    jnp.where(li == 2.0, r0, jnp.where(li == 3.0, r1, 0.0))))
    idx_ref[...] = rec.T[0:8, :]
    gate_ref[...] = jnp.where(li == 0.0, g0, jnp.where(li == 1.0, g1, 0.0))
    cnt_ref[...] = cnt_s[...]


def _router(hf, nw, grp_w, grp_b, exp_w, exp_b):
    t, d = hf.shape
    tm = ROUTER_ROWS
    wr = jnp.zeros((d, LANES), F32)
    wr = wr.at[:, :N_EXPERTS].set(exp_w.reshape(d, N_EXPERTS))
    wr = wr.at[:, GROUP_LANE0:GROUP_LANE0 + N_GROUPS].set(grp_w)
    br = jnp.zeros((1, LANES), F32)
    br = br.at[0, :N_EXPERTS].set(exp_b.reshape(N_EXPERTS))
    br = br.at[0, GROUP_LANE0:GROUP_LANE0 + N_GROUPS].set(grp_b)
    return pl.pallas_call(
        _router_kernel,
        grid=(t // tm,),
        in_specs=[
            pl.BlockSpec((tm, d), lambda i: (i, 0)),
            pl.BlockSpec((1, d), lambda i: (0, 0)),
            pl.BlockSpec((d, LANES), lambda i: (0, 0)),
            pl.BlockSpec((1, LANES), lambda i: (0, 0)),
        ],
        out_specs=[
            pl.BlockSpec((8, tm), lambda i: (0, i)),
            pl.BlockSpec((tm, LANES), lambda i: (i, 0)),
            pl.BlockSpec((1, LANES), lambda i: (0, 0)),
        ],
        out_shape=[
            jax.ShapeDtypeStruct((8, t), F32),
            jax.ShapeDtypeStruct((t, LANES), F32),
            jax.ShapeDtypeStruct((1, LANES), F32),
        ],
        scratch_shapes=[pltpu.VMEM((1, LANES), F32)],
        compiler_params=pltpu.CompilerParams(
            dimension_semantics=("arbitrary",), vmem_limit_bytes=VMEM_LIMIT),
        name="moe_router",
    )(hf, nw.reshape(1, d), wr, br)


def _dispatch_kernel(d0_ref, d1_ref, zf_ref, h_ref, nw_ref, xs_hbm, stage, zbuf, sem, zsem):
    i = pl.program_id(0)
    nb = pl.num_programs(0)
    tm = h_ref.shape[0]
    blk = zbuf.shape[0]
    slot = i % 2

    def wait_slot(s):
        for _ in range(2):
            pltpu.make_async_copy(stage.at[s], xs_hbm.at[pl.ds(0, tm)], sem.at[s]).wait()

    @pl.when(i == 0)
    def _():
        zbuf[...] = jnp.zeros_like(zbuf)

        def zbody(e, carry):
            @pl.when(zf_ref[e] >= 0)
            def _():
                start = pl.multiple_of(zf_ref[e], blk)
                cp = pltpu.make_async_copy(zbuf, xs_hbm.at[pl.ds(start, blk)], zsem)
                cp.start()
                cp.wait()
            return carry
        lax.fori_loop(0, N_EXPERTS, zbody, 0)

        def tbody(b, carry):
            cp = pltpu.make_async_copy(zbuf, xs_hbm.at[pl.ds(pl.multiple_of(b * blk, blk), blk)], zsem)
            cp.start()
            cp.wait()
            return carry
        lax.fori_loop(zf_ref[N_EXPERTS], xs_hbm.shape[0] // blk, tbody, 0)

    @pl.when(i >= 2)
    def _():
        wait_slot(slot)

    stage[slot] = _rms(h_ref[...], nw_ref[...])

    def body(rr, carry):
        for j in range(DMA_UNROLL):
            r = rr * DMA_UNROLL + j
            t = i * tm + r
            src = stage.at[slot, pl.ds(r, 1)]
            pltpu.make_async_copy(src, xs_hbm.at[pl.ds(d0_ref[t], 1)], sem.at[slot]).start()
            pltpu.make_async_copy(src, xs_hbm.at[pl.ds(d1_ref[t], 1)], sem.at[slot]).start()
        return carry
    lax.fori_loop(0, tm // DMA_UNROLL, body, 0)

    @pl.when(i == nb - 1)
    def _():
        wait_slot(slot)

        @pl.when(nb >= 2)
        def _():
            wait_slot(1 - slot)


def _dispatch(hf, nw, dest0, dest1, zfill, n_rows):
    t, d = hf.shape
    tm = DISPATCH_ROWS
    grid_spec = pltpu.PrefetchScalarGridSpec(
        num_scalar_prefetch=3,
        grid=(t // tm,),
        in_specs=[
            pl.BlockSpec((tm, d), lambda i, a, b, c: (i, 0)),
            pl.BlockSpec((1, d), lambda i, a, b, c: (0, 0)),
        ],
        out_specs=pl.BlockSpec(memory_space=pl.ANY),
        scratch_shapes=[pltpu.VMEM((2, tm, d), F32), pltpu.VMEM((MOE_BLOCK, d), F32),
                        pltpu.SemaphoreType.DMA((2,)), pltpu.SemaphoreType.DMA(())],
    )
    return pl.pallas_call(
        _dispatch_kernel,
        grid_spec=grid_spec,
        out_shape=jax.ShapeDtypeStruct((n_rows, d), F32),
        compiler_params=pltpu.CompilerParams(
            dimension_semantics=("arbitrary",), vmem_limit_bytes=VMEM_LIMIT),
        name="moe_dispatch",
    )(dest0, dest1, zfill, hf, nw.reshape(1, d))


def _expert_kernel(be_ref, nu_ref, x_ref, wg_ref, wu_ref, wd_ref, y_ref, wg_s, wu_s, wd_s):
    i = pl.program_id(0)
    prev = be_ref[jnp.maximum(i - 1, 0)]

    @pl.when(i < nu_ref[0])
    def _():
        @pl.when((i == 0) | (be_ref[i] != prev))
        def _():
            wg_s[...] = wg_ref[0, 0].astype(BF16)
            wu_s[...] = wu_ref[0, 0].astype(BF16)
            wd_s[...] = wd_ref[0, 0].astype(BF16)

        hn = x_ref[...].astype(BF16)
        g = jnp.dot(hn, wg_s[...], preferred_element_type=F32)
        u = jnp.dot(hn, wu_s[...], preferred_element_type=F32)
        a = (g * _sigmoid(g) * u).astype(BF16)
        y_ref[...] = jnp.dot(a, wd_s[...], preferred_element_type=F32)

    @pl.when(i >= nu_ref[0])
    def _():
        y_ref[...] = jnp.zeros_like(y_ref)


def _experts(xs, block_expert, n_used, w_gate, w_up, w_down, layer):
    n_rows, d = xs.shape
    blk = MOE_BLOCK
    de = w_gate.shape[-1]
    used_blk = lambda i, be, nu: (jnp.minimum(i, nu[0] - 1), 0)
    grid_spec = pltpu.PrefetchScalarGridSpec(
        num_scalar_prefetch=2,
        grid=(n_rows // blk,),
        in_specs=[
            pl.BlockSpec((blk, d), used_blk),
            pl.BlockSpec((1, 1, d, de), lambda i, be, nu: (layer, be[i], 0, 0)),
            pl.BlockSpec((1, 1, d, de), lambda i, be, nu: (layer, be[i], 0, 0)),
            pl.BlockSpec((1, 1, de, d), lambda i, be, nu: (layer, be[i], 0, 0)),
        ],
        out_specs=pl.BlockSpec((blk, d), lambda i, be, nu: (i, 0)),
        scratch_shapes=[pltpu.VMEM((d, de), BF16), pltpu.VMEM((d, de), BF16),
                        pltpu.VMEM((de, d), BF16)],
    )
    return pl.pallas_call(
        _expert_kernel,
        grid_spec=grid_spec,
        out_shape=jax.ShapeDtypeStruct((n_rows, d), F32),
        compiler_params=pltpu.CompilerParams(
            dimension_semantics=("arbitrary",), vmem_limit_bytes=VMEM_LIMIT),
        name="moe_experts",
    )(block_expert, n_used, xs, w_gate, w_up, w_down)


def _combine_kernel(d0_ref, d1_ref, y_hbm, h_ref, gate_ref, fw_ref, o_ref, ybuf, sem, *, final_norm):
    i = pl.program_id(0)
    nb = pl.num_programs(0)
    tc = h_ref.shape[0]

    def issue(b, slot):
        def body(rr, carry):
            for j in range(DMA_UNROLL):
                r = rr * DMA_UNROLL + j
                pltpu.make_async_copy(y_hbm.at[pl.ds(d0_ref[b * tc + r], 1)],
                                      ybuf.at[slot, 0, pl.ds(r, 1)], sem.at[slot]).start()
                pltpu.make_async_copy(y_hbm.at[pl.ds(d1_ref[b * tc + r], 1)],
                                      ybuf.at[slot, 1, pl.ds(r, 1)], sem.at[slot]).start()
            return carry
        lax.fori_loop(0, tc // DMA_UNROLL, body, 0)

    @pl.when(i == 0)
    def _():
        issue(0, 0)

    @pl.when(i + 1 < nb)
    def _():
        issue(i + 1, (i + 1) % 2)

    slot = i % 2
    for kk in range(2):
        pltpu.make_async_copy(y_hbm.at[pl.ds(0, tc)], ybuf.at[slot, kk], sem.at[slot]).wait()

    gate = gate_ref[...]
    o = h_ref[...] + (gate[:, 0:1] * ybuf[slot, 0] + gate[:, 1:2] * ybuf[slot, 1])
    if final_norm:
        o = _rms(o, fw_ref[...])
    o_ref[...] = o


def _combine(hf, y, dest0, dest1, gates, fw, final_norm):
    t, d = hf.shape
    tc = COMBINE_ROWS
    grid_spec = pltpu.PrefetchScalarGridSpec(
        num_scalar_prefetch=2,
        grid=(t // tc,),
        in_specs=[
            pl.BlockSpec(memory_space=pl.ANY),
            pl.BlockSpec((tc, d), lambda i, a, b: (i, 0)),
            pl.BlockSpec((tc, LANES), lambda i, a, b: (i, 0)),
            pl.BlockSpec((1, d), lambda i, a, b: (0, 0)),
        ],
        out_specs=pl.BlockSpec((tc, d), lambda i, a, b: (i, 0)),
        scratch_shapes=[pltpu.VMEM((2, 2, tc, d), F32), pltpu.SemaphoreType.DMA((2,))],
    )
    return pl.pallas_call(
        functools.partial(_combine_kernel, final_norm=final_norm),
        grid_spec=grid_spec,
        out_shape=jax.ShapeDtypeStruct((t, d), F32),
        compiler_params=pltpu.CompilerParams(
            dimension_semantics=("arbitrary",), vmem_limit_bytes=VMEM_LIMIT),
        name="moe_combine",
    )(dest0, dest1, y, hf, gates, fw.reshape(1, d))


def _moe_layer(h, nw, grp_w, grp_b, exp_w, exp_b, w_gate, w_up, w_down, fw, layer, final_norm):
    b, s, d = h.shape
    t = b * s
    hf = h.reshape(t, d)
    rec, gates, cnt = _router(hf, nw, grp_w, grp_b, exp_w, exp_b)

    rec = rec.astype(jnp.int32)
    counts = cnt[0, :N_EXPERTS].astype(jnp.int32)
    padded = (counts + MOE_BLOCK - 1) // MOE_BLOCK * MOE_BLOCK
    pad_end = jnp.cumsum(padded)
    pad_start = pad_end - padded
    eids = jnp.arange(N_EXPERTS, dtype=jnp.int32)[:, None]
    dest0 = jnp.sum(jnp.where(rec[0][None, :] == eids, pad_start[:, None], 0), axis=0) + rec[2]
    dest1 = jnp.sum(jnp.where(rec[1][None, :] == eids, pad_start[:, None], 0), axis=0) + rec[3]
    n_rows = 2 * t + N_EXPERTS * MOE_BLOCK
    n_blocks = n_rows // MOE_BLOCK
    block_start = jnp.arange(n_blocks, dtype=jnp.int32) * MOE_BLOCK
    block_expert = jnp.minimum(
        jnp.sum((block_start[:, None] >= pad_end[None, :]).astype(jnp.int32), axis=1), N_EXPERTS - 1)
    n_used = (pad_end[-1:] // MOE_BLOCK).astype(jnp.int32)
    zfill = jnp.concatenate([jnp.where(padded > 0, pad_end - MOE_BLOCK, -1).astype(jnp.int32), n_used])

    xs = _dispatch(hf, nw, dest0, dest1, zfill, n_rows)
    y = _experts(xs, block_expert, n_used, w_gate, w_up, w_down, layer)
    out = _combine(hf, y, dest0, dest1, gates, fw, final_norm)
    return out.reshape(b, s, d)


def kernel(x, conv_norm_w, conv_pw1_w, conv_pw1_b, conv_dw_w, conv_dw_b, conv_ln_g, conv_ln_b,
           conv_pw2_w, conv_pw2_b, hgrn_norm_w, hgrn_w_in, hgrn_gnorm_w, hgrn_w_out, lower_bounds,
           ffn_norm_w, router_grp_w, router_grp_b, router_exp_w, router_exp_b,
           moe_w_gate, moe_w_up, moe_w_down, final_norm_w):
    depth = lower_bounds.shape[0]
    lb_p = jax.nn.softmax(lower_bounds.astype(F32), axis=0)
    lb_all = jnp.cumsum(lb_p, axis=0) - lb_p[0]
    h = x
    for layer in range(depth):
        j = layer // 2
        if layer % 2 == 0:
            h = _conformer_layer(h, conv_norm_w[j], conv_pw1_w[j], conv_pw1_b[j], conv_dw_w[j],
                                 conv_dw_b[j], conv_ln_g[j], conv_ln_b[j], conv_pw2_w[j],
                                 conv_pw2_b[j])
        else:
            h = _hgrn_layer(h, hgrn_norm_w[j], hgrn_w_in[j], hgrn_gnorm_w[j], hgrn_w_out[j],
                            lb_all[layer])
        h = _moe_layer(h, ffn_norm_w[layer], router_grp_w[layer], router_grp_b[layer],
                       router_exp_w[layer], router_exp_b[layer], moe_w_gate, moe_w_up,
                       moe_w_down, final_norm_w, layer=layer, final_norm=(layer == depth - 1))
    return h
```

```python
import functools

import jax
import jax.numpy as jnp
from jax import lax
from jax.experimental import pallas as pl
from jax.experimental.pallas import tpu as pltpu

F32 = jnp.float32
BF16 = jnp.bfloat16

NORM_EPS = 1e-6
CONV_WIDTH = 31
CONV_HALO = 32
HGRN_HEADS = 8
HEAD_DIM = 128
CHUNK = 64
N_GROUPS = 4
EXPERTS_PER_GROUP = 8
N_EXPERTS = N_GROUPS * EXPERTS_PER_GROUP
LANES = 128
SUBLANES = 8
GROUP_LANE0 = N_EXPERTS

CONV_ROWS = 256
CONV_RC = 128
HGRN_ROWS = 1024
ROUTER_ROWS = 512
MOE_BLOCK = 256
DMA_UNROLL = 4
DISPATCH_ROWS = 256
COMBINE_ROWS = 256
VMEM_LIMIT = 48 * 1024 * 1024


def _rms(x, w):
    ms = jnp.mean(x * x, axis=-1, keepdims=True)
    return x * lax.rsqrt(ms + NORM_EPS) * w


def _sigmoid(x):
    return 1.0 / (1.0 + jnp.exp(-x))


def _conv_kernel(x_ref, nw_ref, pw1_ref, b1_ref, dw_ref, dwb_ref, lng_ref, lnb_ref,
                 pw2_ref, b2_ref, o_ref, ubuf, cbuf):
    s = pl.program_id(1)
    ts = x_ref.shape[1]
    d = x_ref.shape[2]
    x = x_ref[0]
    hn = _rms(x, nw_ref[...]).astype(BF16)
    a = jnp.dot(hn, pw1_ref[...], preferred_element_type=F32) + b1_ref[...]
    u = a[:, :d] * _sigmoid(a[:, d:])

    @pl.when(s == 0)
    def _():
        ubuf[0:CONV_HALO, :] = jnp.zeros((CONV_HALO, d), F32)

    @pl.when(s > 0)
    def _():
        ubuf[0:CONV_HALO, :] = ubuf[ts:ts + CONV_HALO, :]

    ubuf[CONV_HALO:, :] = u

    off0 = CONV_HALO - (CONV_WIDTH - 1)
    win = CONV_RC + CONV_HALO
    for rc in range(ts // CONV_RC):
        for lc in range(d // LANES):
            ls = slice(lc * LANES, (lc + 1) * LANES)
            xw = ubuf[rc * CONV_RC:rc * CONV_RC + win, ls]
            acc = jnp.broadcast_to(dwb_ref[:, ls], (CONV_RC, LANES))
            for p in range(SUBLANES):
                xp = xw if p == 0 else pltpu.roll(xw, win - p, axis=0)
                for j in range(CONV_WIDTH):
                    if (off0 + j) % SUBLANES == p:
                        a0 = off0 + j - p
                        acc = acc + dw_ref[j:j + 1, ls] * xp[a0:a0 + CONV_RC, :]
            cbuf[rc * CONV_RC:(rc + 1) * CONV_RC, ls] = acc

    c = cbuf[...]
    mu = jnp.mean(c, axis=-1, keepdims=True)
    cc = c - mu
    var = jnp.mean(cc * cc, axis=-1, keepdims=True)
    n = cc * lax.rsqrt(var + NORM_EPS) * lng_ref[...] + lnb_ref[...]
    sw = (n * _sigmoid(n)).astype(BF16)
    y = jnp.dot(sw, pw2_ref[...], preferred_element_type=F32) + b2_ref[...]
    o_ref[0] = x + y


def _conformer_layer(h, nw, pw1, b1, dw, dwb, lng, lnb, pw2, b2):
    b, s, d = h.shape
    ts = CONV_ROWS
    row = lambda v: v.reshape(1, -1)
    full = lambda shape: pl.BlockSpec(shape, lambda bi, si: (0,) * len(shape))
    return pl.pallas_call(
        _conv_kernel,
        grid=(b, s // ts),
        in_specs=[
            pl.BlockSpec((1, ts, d), lambda bi, si: (bi, si, 0)),
            full((1, d)), full((d, 2 * d)), full((1, 2 * d)),
            full((CONV_WIDTH, d)), full((1, d)), full((1, d)), full((1, d)),
            full((d, d)), full((1, d)),
        ],
        out_specs=pl.BlockSpec((1, ts, d), lambda bi, si: (bi, si, 0)),
        out_shape=jax.ShapeDtypeStruct((b, s, d), F32),
        scratch_shapes=[pltpu.VMEM((ts + CONV_HALO, d), F32), pltpu.VMEM((ts, d), F32)],
        compiler_params=pltpu.CompilerParams(
            dimension_semantics=("arbitrary", "arbitrary"), vmem_limit_bytes=VMEM_LIMIT),
        name="conformer_conv",
    )(h, row(nw), pw1.astype(BF16), row(b1), dw, row(dwb), row(lng), row(lnb),
      pw2.astype(BF16), row(b2))


HGRN_GROUP = 4
GROUP_ROWS = HGRN_GROUP * CHUNK


def _hgrn_state_updates(gi, k_s, v_s, g_s, upd_s):
    rows = pl.ds(pl.multiple_of(gi * GROUP_ROWS, GROUP_ROWS), GROUP_ROWS)
    gc = g_s[rows, :]
    k = k_s[rows, :]
    chunk_of_row = lax.broadcasted_iota(jnp.int32, (GROUP_ROWS, HEAD_DIM), 0) // CHUNK
    glast = jnp.concatenate(
        [jnp.broadcast_to(gc[(j + 1) * CHUNK - 1:(j + 1) * CHUNK, :], (CHUNK, HEAD_DIM))
         for j in range(HGRN_GROUP)], axis=0)
    kd = k * jnp.exp(glast - gc)
    rhs = jnp.concatenate([jnp.where(chunk_of_row == j, kd, 0.0).astype(BF16)
                           for j in range(HGRN_GROUP)], axis=1)
    vb = v_s[rows, :].astype(BF16)
    upd = lax.dot_general(vb, rhs, (((0,), (0,)), ((), ())), preferred_element_type=F32)
    for j in range(HGRN_GROUP):
        upd_s[gi * HGRN_GROUP + j] = upd[:, j * HEAD_DIM:(j + 1) * HEAD_DIM]


def _hgrn_group(gi, q_s, k_s, v_s, g_s, o_s, sb_s):
    n = GROUP_ROWS
    rows = pl.ds(pl.multiple_of(gi * n, n), n)
    q = q_s[rows, :]
    k = k_s[rows, :]
    gc = g_s[rows, :]
    vb = v_s[rows, :].astype(BF16)
    nt = (((1,), (1,)), ((), ()))

    qg = (q * jnp.exp(gc)).astype(BF16)
    sts = sb_s[pl.ds(pl.multiple_of(gi * HGRN_GROUP * HEAD_DIM, HGRN_GROUP * HEAD_DIM),
                     HGRN_GROUP * HEAD_DIM), :]
    wide = lax.dot_general(qg, sts, nt, preferred_element_type=F32)
    inter = jnp.concatenate(
        [wide[j * CHUNK:(j + 1) * CHUNK, j * HEAD_DIM:(j + 1) * HEAD_DIM]
         for j in range(HGRN_GROUP)], axis=0)

    row = lax.broadcasted_iota(jnp.int32, (n, HEAD_DIM), 0)
    ti = lax.broadcasted_iota(jnp.int32, (n, n), 0)
    si = lax.broadcasted_iota(jnp.int32, (n, n), 1)
    tx = ti ^ si
    scores = jnp.broadcast_to(jnp.sum(q * k, axis=-1, keepdims=True), (n, n))
    end = gc
    b = 1
    while b < CHUNK:
        upper = (row & b) != 0
        e = jnp.exp(jnp.where(upper, gc - pltpu.roll(end, b, axis=0), end - gc))
        m = (jnp.where(upper, q, k) * e).astype(BF16)
        sb = lax.dot_general(m, m, nt, preferred_element_type=F32)
        scores = jnp.where(tx >= b, sb, scores)
        end = jnp.where(upper, end, pltpu.roll(end, n - b, axis=0))
        b *= 2
    scores = jnp.where((ti >= si) & (tx < CHUNK), scores, 0.0)

    intra = jnp.dot(scores.astype(BF16), vb, preferred_element_type=F32)
    o_s[rows, :] = inter + intra


def _hgrn_kernel(x_ref, nw_ref, win_ref, lb_ref, gn_ref, wout_ref, o_ref,
                 hn_s, proj_s, q_s, k_s, v_s, g_s, z_s, o_s, oh_s, upd_s, sb_s, st_ref):
    s = pl.program_id(1)
    h = pl.program_id(2)
    nh = pl.num_programs(2)
    ts = x_ref.shape[1]

    @pl.when(h == 0)
    def _():
        hn_s[...] = _rms(x_ref[0], nw_ref[...]).astype(BF16)
        proj_s[...] = jnp.dot(hn_s[...], win_ref[0], preferred_element_type=F32)

    @pl.when(s == 0)
    def _():
        st_ref[h] = jnp.zeros((HEAD_DIM, HEAD_DIM), F32)

    proj = proj_s[...]
    qr = proj[:, 0:HEAD_DIM]
    q_s[...] = qr * _sigmoid(qr)
    lb = lb_ref[0]
    forget = lb + (1.0 - lb) * _sigmoid(proj[:, HEAD_DIM:2 * HEAD_DIM])
    k_s[...] = 1.0 - forget
    v_s[...] = proj[:, 2 * HEAD_DIM:3 * HEAD_DIM]
    zr = proj[:, 3 * HEAD_DIM:4 * HEAD_DIM]
    z_s[...] = zr * _sigmoid(zr)

    g = jnp.log(forget)
    rmod = lax.broadcasted_iota(jnp.int32, (ts, 1), 0) % CHUNK
    dstep = 1
    while dstep < CHUNK:
        g = g + jnp.where(rmod >= dstep, pltpu.roll(g, dstep, axis=0), 0.0)
        dstep *= 2
    g_s[...] = g

    proj_s[...] = jnp.dot(hn_s[...], win_ref[jnp.minimum(h + 1, nh - 1)],
                          preferred_element_type=F32)

    nchunk = ts // CHUNK
    ngroup = ts // GROUP_ROWS
    for gi in range(ngroup):
        _hgrn_state_updates(gi, k_s, v_s, g_s, upd_s)

    st = st_ref[h]
    for c in range(nchunk):
        sb_s[c * HEAD_DIM:(c + 1) * HEAD_DIM, :] = st.astype(BF16)
        glast = g_s[(c + 1) * CHUNK - 1:(c + 1) * CHUNK, :]
        st = jnp.exp(glast) * st + upd_s[c]
    st_ref[h] = st

    for gi in range(ngroup):
        _hgrn_group(gi, q_s, k_s, v_s, g_s, o_s, sb_s)

    o = o_s[...]
    o = o * lax.rsqrt(jnp.mean(o * o, axis=-1, keepdims=True) + NORM_EPS) * gn_ref[...]
    oh_s[h] = (o * z_s[...]).astype(BF16)

    @pl.when(h == pl.num_programs(2) - 1)
    def _():
        oall = jnp.concatenate([oh_s[hh] for hh in range(HGRN_HEADS)], axis=1)
        o_ref[0] = x_ref[0] + jnp.dot(oall, wout_ref[...], preferred_element_type=F32)


def _hgrn_layer(h, nw, w_in, gn, w_out, lb):
    b, s, d = h.shape
    ts = HGRN_ROWS
    nh = HGRN_HEADS
    win_r = w_in.reshape(d, 4, nh, HEAD_DIM).transpose(2, 0, 1, 3).reshape(nh, d, 4 * HEAD_DIM)
    return pl.pallas_call(
        _hgrn_kernel,
        grid=(b, s // ts, nh),
        in_specs=[
            pl.BlockSpec((1, ts, d), lambda bi, si, hi: (bi, si, 0)),
            pl.BlockSpec((1, d), lambda bi, si, hi: (0, 0)),
            pl.BlockSpec((nh, d, 4 * HEAD_DIM), lambda bi, si, hi: (0, 0, 0),
                         pipeline_mode=pl.Buffered(1)),
            pl.BlockSpec((1, 1, HEAD_DIM), lambda bi, si, hi: (hi, 0, 0)),
            pl.BlockSpec((1, HEAD_DIM), lambda bi, si, hi: (0, 0)),
            pl.BlockSpec((d, d), lambda bi, si, hi: (0, 0), pipeline_mode=pl.Buffered(1)),
        ],
        out_specs=pl.BlockSpec((1, ts, d), lambda bi, si, hi: (bi, si, 0)),
        out_shape=jax.ShapeDtypeStruct((b, s, d), F32),
        scratch_shapes=[
            pltpu.VMEM((ts, d), BF16),
            pltpu.VMEM((ts, 4 * HEAD_DIM), F32),
            pltpu.VMEM((ts, HEAD_DIM), F32), pltpu.VMEM((ts, HEAD_DIM), F32),
            pltpu.VMEM((ts, HEAD_DIM), F32), pltpu.VMEM((ts, HEAD_DIM), F32),
            pltpu.VMEM((ts, HEAD_DIM), F32), pltpu.VMEM((ts, HEAD_DIM), F32),
            pltpu.VMEM((nh, ts, HEAD_DIM), BF16),
            pltpu.VMEM((ts // CHUNK, HEAD_DIM, HEAD_DIM), F32),
            pltpu.VMEM((ts // CHUNK * HEAD_DIM, HEAD_DIM), BF16),
            pltpu.VMEM((nh, HEAD_DIM, HEAD_DIM), F32),
        ],
        compiler_params=pltpu.CompilerParams(
            dimension_semantics=("arbitrary", "arbitrary", "arbitrary"),
            vmem_limit_bytes=VMEM_LIMIT),
        name="hgrn2",
    )(h, nw.reshape(1, d), win_r.astype(BF16), lb.reshape(nh, 1, HEAD_DIM),
      gn.reshape(1, HEAD_DIM), w_out.astype(BF16))


def _router_kernel(h_ref, nw_ref, wr_ref, br_ref, idx_ref, gate_ref, cnt_ref, cnt_s):
    i = pl.program_id(0)
    tm = h_ref.shape[0]

    @pl.when(i == 0)
    def _():
        cnt_s[...] = jnp.zeros_like(cnt_s)

    hn = _rms(h_ref[...], nw_ref[...])
    logits = jnp.dot(hn, wr_ref[...], preferred_element_type=F32,
                     precision=lax.Precision.HIGHEST) + br_ref[...]
    li = lax.broadcasted_iota(jnp.int32, (tm, LANES), 1).astype(F32)
    neg = jnp.float32(-jnp.inf)
    big = jnp.float32(LANES)

    gl = jnp.where((li >= GROUP_LANE0) & (li < GROUP_LANE0 + N_GROUPS), logits, neg)
    gmax = jnp.max(gl, axis=-1, keepdims=True)
    gval = 1.0 / jnp.sum(jnp.exp(gl - gmax), axis=-1, keepdims=True)
    gidx = jnp.min(jnp.where(gl == gmax, li, big), axis=-1, keepdims=True) - GROUP_LANE0

    lo = gidx * EXPERTS_PER_GROUP
    el = jnp.where((li >= lo) & (li < lo + EXPERTS_PER_GROUP), logits, neg)
    v0 = jnp.max(el, axis=-1, keepdims=True)
    i0 = jnp.min(jnp.where(el == v0, li, big), axis=-1, keepdims=True)
    el2 = jnp.where(li == i0, neg, el)
    v1 = jnp.max(el2, axis=-1, keepdims=True)
    i1 = jnp.min(jnp.where(el2 == v1, li, big), axis=-1, keepdims=True)
    t = jnp.exp(v1 - v0)
    p0 = 1.0 / (1.0 + t)
    g0 = gval * p0
    g1 = gval * (t * p0)

    oh = jnp.where((li == i0) | (li == i1), 1.0, 0.0)
    rr = lax.broadcasted_iota(jnp.int32, (tm, tm), 0)
    cc = lax.broadcasted_iota(jnp.int32, (tm, tm), 1)
    lower = jnp.where(rr > cc, 1.0, 0.0).astype(BF16)
    before = jnp.dot(lower, oh.astype(BF16), preferred_element_type=F32) + cnt_s[...]
    r0 = jnp.sum(jnp.where(li == i0, before, 0.0), axis=-1, keepdims=True)
    r1 = jnp.sum(jnp.where(li == i1, before, 0.0), axis=-1, keepdims=True)
    cnt_s[...] = cnt_s[...] + jnp.sum(oh, axis=0, keepdims=True)

    rec = jnp.where(li == 0.0, i0, jnp.where(li == 1.0, i1,
                    jnp.where(li == 2.0, r0, jnp.where(li == 3.0, r1, 0.0))))
    idx_ref[...] = rec.T[0:8, :]
    gate_ref[...] = jnp.where(li == 0.0, g0, jnp.where(li == 1.0, g1, 0.0))
    cnt_ref[...] = cnt_s[...]


def _router(hf, nw, grp_w, grp_b, exp_w, exp_b):
    t, d = hf.shape
    tm = ROUTER_ROWS
    wr = jnp.zeros((d, LANES), F32)
    wr = wr.at[:, :N_EXPERTS].set(exp_w.reshape(d, N_EXPERTS))
    wr = wr.at[:, GROUP_LANE0:GROUP_LANE0 + N_GROUPS].set(grp_w)
    br = jnp.zeros((1, LANES), F32)
    br = br.at[0, :N_EXPERTS].set(exp_b.reshape(N_EXPERTS))
    br = br.at[0, GROUP_LANE0:GROUP_LANE0 + N_GROUPS].set(grp_b)
    return pl.pallas_call(
        _router_kernel,
        grid=(t // tm,),
        in_specs=[
            pl.BlockSpec((tm, d), lambda i: (i, 0)),
            pl.BlockSpec((1, d), lambda i: (0, 0)),
            pl.BlockSpec((d, LANES), lambda i: (0, 0)),
            pl.BlockSpec((1, LANES), lambda i: (0, 0)),
        ],
        out_specs=[
            pl.BlockSpec((8, tm), lambda i: (0, i)),
            pl.BlockSpec((tm, LANES), lambda i: (i, 0)),
            pl.BlockSpec((1, LANES), lambda i: (0, 0)),
        ],
        out_shape=[
            jax.ShapeDtypeStruct((8, t), F32),
            jax.ShapeDtypeStruct((t, LANES), F32),
            jax.ShapeDtypeStruct((1, LANES), F32),
        ],
        scratch_shapes=[pltpu.VMEM((1, LANES), F32)],
        compiler_params=pltpu.CompilerParams(
            dimension_semantics=("arbitrary",), vmem_limit_bytes=VMEM_LIMIT),
        name="moe_router",
    )(hf, nw.reshape(1, d), wr, br)


def _dispatch_kernel(d0_ref, d1_ref, zf_ref, h_ref, nw_ref, xs_hbm, stage, zbuf, sem, zsem):
    i = pl.program_id(0)
    nb = pl.num_programs(0)
    tm = h_ref.shape[0]
    blk = zbuf.shape[0]
    slot = i % 2

    def wait_slot(s):
        for _ in range(2):
            pltpu.make_async_copy(stage.at[s], xs_hbm.at[pl.ds(0, tm)], sem.at[s]).wait()

    @pl.when(i == 0)
    def _():
        zbuf[...] = jnp.zeros_like(zbuf)

        def zbody(e, carry):
            @pl.when(zf_ref[e] >= 0)
            def _():
                start = pl.multiple_of(zf_ref[e], blk)
                cp = pltpu.make_async_copy(zbuf, xs_hbm.at[pl.ds(start, blk)], zsem)
                cp.start()
                cp.wait()
            return carry
        lax.fori_loop(0, N_EXPERTS, zbody, 0)

        def tbody(b, carry):
            cp = pltpu.make_async_copy(zbuf, xs_hbm.at[pl.ds(pl.multiple_of(b * blk, blk), blk)], zsem)
            cp.start()
            cp.wait()
            return carry
        lax.fori_loop(zf_ref[N_EXPERTS], xs_hbm.shape[0] // blk, tbody, 0)

    @pl.when(i >= 2)
    def _():
        wait_slot(slot)

    stage[slot] = _rms(h_ref[...], nw_ref[...])

    def body(rr, carry):
        for j in range(DMA_UNROLL):
            r = rr * DMA_UNROLL + j
            t = i * tm + r
            src = stage.at[slot, pl.ds(r, 1)]
            pltpu.make_async_copy(src, xs_hbm.at[pl.ds(d0_ref[t], 1)], sem.at[slot]).start()
            pltpu.make_async_copy(src, xs_hbm.at[pl.ds(d1_ref[t], 1)], sem.at[slot]).start()
        return carry
    lax.fori_loop(0, tm // DMA_UNROLL, body, 0)

    @pl.when(i == nb - 1)
    def _():
        wait_slot(slot)

        @pl.when(nb >= 2)
        def _():
            wait_slot(1 - slot)


def _dispatch(hf, nw, dest0, dest1, zfill, n_rows):
    t, d = hf.shape
    tm = DISPATCH_ROWS
    grid_spec = pltpu.PrefetchScalarGridSpec(
        num_scalar_prefetch=3,
        grid=(t // tm,),
        in_specs=[
            pl.BlockSpec((tm, d), lambda i, a, b, c: (i, 0)),
            pl.BlockSpec((1, d), lambda i, a, b, c: (0, 0)),
        ],
        out_specs=pl.BlockSpec(memory_space=pl.ANY),
        scratch_shapes=[pltpu.VMEM((2, tm, d), F32), pltpu.VMEM((MOE_BLOCK, d), F32),
                        pltpu.SemaphoreType.DMA((2,)), pltpu.SemaphoreType.DMA(())],
    )
    return pl.pallas_call(
        _dispatch_kernel,
        grid_spec=grid_spec,
        out_shape=jax.ShapeDtypeStruct((n_rows, d), F32),
        compiler_params=pltpu.CompilerParams(
            dimension_semantics=("arbitrary",), vmem_limit_bytes=VMEM_LIMIT),
        name="moe_dispatch",
    )(dest0, dest1, zfill, hf, nw.reshape(1, d))


def _expert_kernel(be_ref, nu_ref, x_ref, wg_ref, wu_ref, wd_ref, y_ref, wg_s, wu_s, wd_s):
    i = pl.program_id(0)
    prev = be_ref[jnp.maximum(i - 1, 0)]

    @pl.when(i < nu_ref[0])
    def _():
        @pl.when((i == 0) | (be_ref[i] != prev))
        def _():
            wg_s[...] = wg_ref[0, 0].astype(BF16)
            wu_s[...] = wu_ref[0, 0].astype(BF16)
            wd_s[...] = wd_ref[0, 0].astype(BF16)

        hn = x_ref[...].astype(BF16)
        g = jnp.dot(hn, wg_s[...], preferred_element_type=F32)
        u = jnp.dot(hn, wu_s[...], preferred_element_type=F32)
        a = (g * _sigmoid(g) * u).astype(BF16)
        y_ref[...] = jnp.dot(a, wd_s[...], preferred_element_type=F32)

    @pl.when(i >= nu_ref[0])
    def _():
        y_ref[...] = jnp.zeros_like(y_ref)


def _experts(xs, block_expert, n_used, w_gate, w_up, w_down, layer):
    n_rows, d = xs.shape
    blk = MOE_BLOCK
    de = w_gate.shape[-1]
    used_blk = lambda i, be, nu: (jnp.minimum(i, nu[0] - 1), 0)
    grid_spec = pltpu.PrefetchScalarGridSpec(
        num_scalar_prefetch=2,
        grid=(n_rows // blk,),
        in_specs=[
            pl.BlockSpec((blk, d), used_blk),
            pl.BlockSpec((1, 1, d, de), lambda i, be, nu: (layer, be[i], 0, 0)),
            pl.BlockSpec((1, 1, d, de), lambda i, be, nu: (layer, be[i], 0, 0)),
            pl.BlockSpec((1, 1, de, d), lambda i, be, nu: (layer, be[i], 0, 0)),
        ],
        out_specs=pl.BlockSpec((blk, d), lambda i, be, nu: (i, 0)),
        scratch_shapes=[pltpu.VMEM((d, de), BF16), pltpu.VMEM((d, de), BF16),
                        pltpu.VMEM((de, d), BF16)],
    )
    return pl.pallas_call(
        _expert_kernel,
        grid_spec=grid_spec,
        out_shape=jax.ShapeDtypeStruct((n_rows, d), F32),
        compiler_params=pltpu.CompilerParams(
            dimension_semantics=("arbitrary",), vmem_limit_bytes=VMEM_LIMIT),
        name="moe_experts",
    )(block_expert, n_used, xs, w_gate, w_up, w_down)


def _combine_kernel(d0_ref, d1_ref, y_hbm, h_ref, gate_ref, fw_ref, o_ref, ybuf, sem, *, final_norm):
    i = pl.program_id(0)
    nb = pl.num_programs(0)
    tc = h_ref.shape[0]

    def issue(b, slot):
        def body(rr, carry):
            for j in range(DMA_UNROLL):
                r = rr * DMA_UNROLL + j
                pltpu.make_async_copy(y_hbm.at[pl.ds(d0_ref[b * tc + r], 1)],
                                      ybuf.at[slot, 0, pl.ds(r, 1)], sem.at[slot]).start()
                pltpu.make_async_copy(y_hbm.at[pl.ds(d1_ref[b * tc + r], 1)],
                                      ybuf.at[slot, 1, pl.ds(r, 1)], sem.at[slot]).start()
            return carry
        lax.fori_loop(0, tc // DMA_UNROLL, body, 0)

    @pl.when(i == 0)
    def _():
        issue(0, 0)

    @pl.when(i + 1 < nb)
    def _():
        issue(i + 1, (i + 1) % 2)

    slot = i % 2
    for kk in range(2):
        pltpu.make_async_copy(y_hbm.at[pl.ds(0, tc)], ybuf.at[slot, kk], sem.at[slot]).wait()

    gate = gate_ref[...]
    o = h_ref[...] + (gate[:, 0:1] * ybuf[slot, 0] + gate[:, 1:2] * ybuf[slot, 1])
    if final_norm:
        o = _rms(o, fw_ref[...])
    o_ref[...] = o


def _combine(hf, y, dest0, dest1, gates, fw, final_norm):
    t, d = hf.shape
    tc = COMBINE_ROWS
    grid_spec = pltpu.PrefetchScalarGridSpec(
        num_scalar_prefetch=2,
        grid=(t // tc,),
        in_specs=[
            pl.BlockSpec(memory_space=pl.ANY),
            pl.BlockSpec((tc, d), lambda i, a, b: (i, 0)),
            pl.BlockSpec((tc, LANES), lambda i, a, b: (i, 0)),
            pl.BlockSpec((1, d), lambda i, a, b: (0, 0)),
        ],
        out_specs=pl.BlockSpec((tc, d), lambda i, a, b: (i, 0)),
        scratch_shapes=[pltpu.VMEM((2, 2, tc, d), F32), pltpu.SemaphoreType.DMA((2,))],
    )
    return pl.pallas_call(
        functools.partial(_combine_kernel, final_norm=final_norm),
        grid_spec=grid_spec,
        out_shape=jax.ShapeDtypeStruct((t, d), F32),
        compiler_params=pltpu.CompilerParams(
            dimension_semantics=("arbitrary",), vmem_limit_bytes=VMEM_LIMIT),
        name="moe_combine",
    )(dest0, dest1, y, hf, gates, fw.reshape(1, d))


def _moe_layer(h, nw, grp_w, grp_b, exp_w, exp_b, w_gate, w_up, w_down, fw, layer, final_norm):
    b, s, d = h.shape
    t = b * s
    hf = h.reshape(t, d)
    rec, gates, cnt = _router(hf, nw, grp_w, grp_b, exp_w, exp_b)

    rec = rec.astype(jnp.int32)
    counts = cnt[0, :N_EXPERTS].astype(jnp.int32)
    padded = (counts + MOE_BLOCK - 1) // MOE_BLOCK * MOE_BLOCK
    pad_end = jnp.cumsum(padded)
    pad_start = pad_end - padded
    eids = jnp.arange(N_EXPERTS, dtype=jnp.int32)[:, None]
    dest0 = jnp.sum(jnp.where(rec[0][None, :] == eids, pad_start[:, None], 0), axis=0) + rec[2]
    dest1 = jnp.sum(jnp.where(rec[1][None, :] == eids, pad_start[:, None], 0), axis=0) + rec[3]
    n_rows = 2 * t + N_EXPERTS * MOE_BLOCK
    n_blocks = n_rows // MOE_BLOCK
    block_start = jnp.arange(n_blocks, dtype=jnp.int32) * MOE_BLOCK
    block_expert = jnp.minimum(
        jnp.sum((block_start[:, None] >= pad_end[None, :]).astype(jnp.int32), axis=1), N_EXPERTS - 1)
    n_used = (pad_end[-1:] // MOE_BLOCK).astype(jnp.int32)
    zfill = jnp.concatenate([jnp.where(padded > 0, pad_end - MOE_BLOCK, -1).astype(jnp.int32), n_used])

    xs = _dispatch(hf, nw, dest0, dest1, zfill, n_rows)
    y = _experts(xs, block_expert, n_used, w_gate, w_up, w_down, layer)
    out = _combine(hf, y, dest0, dest1, gates, fw, final_norm)
    return out.reshape(b, s, d)


def kernel(x, conv_norm_w, conv_pw1_w, conv_pw1_b, conv_dw_w, conv_dw_b, conv_ln_g, conv_ln_b,
           conv_pw2_w, conv_pw2_b, hgrn_norm_w, hgrn_w_in, hgrn_gnorm_w, hgrn_w_out, lower_bounds,
           ffn_norm_w, router_grp_w, router_grp_b, router_exp_w, router_exp_b,
           moe_w_gate, moe_w_up, moe_w_down, final_norm_w):
    depth = lower_bounds.shape[0]
    lb_p = jax.nn.softmax(lower_bounds.astype(F32), axis=0)
    lb_all = jnp.cumsum(lb_p, axis=0) - lb_p[0]
    h = x
    for layer in range(depth):
        j = layer // 2
        if layer % 2 == 0:
            h = _conformer_layer(h, conv_norm_w[j], conv_pw1_w[j], conv_pw1_b[j], conv_dw_w[j],
                                 conv_dw_b[j], conv_ln_g[j], conv_ln_b[j], conv_pw2_w[j],
                                 conv_pw2_b[j])
        else:
            h = _hgrn_layer(h, hgrn_norm_w[j], hgrn_w_in[j], hgrn_gnorm_w[j], hgrn_w_out[j],
                            lb_all[layer])
        h = _moe_layer(h, ffn_norm_w[layer], router_grp_w[layer], router_grp_b[layer],
                       router_exp_w[layer], router_exp_b[layer], moe_w_gate, moe_w_up,
                       moe_w_down, final_norm_w, layer=layer, final_norm=(layer == depth - 1))
    return h
```

```python
import functools

import jax
import jax.numpy as jnp
from jax import lax
from jax.experimental import pallas as pl
from jax.experimental.pallas import tpu as pltpu

F32 = jnp.float32
BF16 = jnp.bfloat16

NORM_EPS = 1e-6
CONV_WIDTH = 31
CONV_HALO = 32
HGRN_HEADS = 8
HEAD_DIM = 128
CHUNK = 64
N_GROUPS = 4
EXPERTS_PER_GROUP = 8
N_EXPERTS = N_GROUPS * EXPERTS_PER_GROUP
LANES = 128
SUBLANES = 8
GROUP_LANE0 = N_EXPERTS

CONV_ROWS = 256
CONV_RC = 128
HGRN_ROWS = 1024
ROUTER_ROWS = 512
MOE_BLOCK = 256
DMA_UNROLL = 8
DISPATCH_ROWS = 512
COMBINE_ROWS = 512
VMEM_LIMIT = 48 * 1024 * 1024


def _rms(x, w):
    ms = jnp.mean(x * x, axis=-1, keepdims=True)
    return x * lax.rsqrt(ms + NORM_EPS) * w


def _sigmoid(x):
    return 1.0 / (1.0 + jnp.exp(-x))


def _conv_kernel(x_ref, nw_ref, pw1_ref, b1_ref, dw_ref, dwb_ref, lng_ref, lnb_ref,
                 pw2_ref, b2_ref, o_ref, ubuf, cbuf):
    s = pl.program_id(1)
    ts = x_ref.shape[1]
    d = x_ref.shape[2]
    x = x_ref[0]
    hn = _rms(x, nw_ref[...]).astype(BF16)
    a = jnp.dot(hn, pw1_ref[...], preferred_element_type=F32) + b1_ref[...]
    u = a[:, :d] * _sigmoid(a[:, d:])

    @pl.when(s == 0)
    def _():
        ubuf[0:CONV_HALO, :] = jnp.zeros((CONV_HALO, d), F32)

    @pl.when(s > 0)
    def _():
        ubuf[0:CONV_HALO, :] = ubuf[ts:ts + CONV_HALO, :]

    ubuf[CONV_HALO:, :] = u

    off0 = CONV_HALO - (CONV_WIDTH - 1)
    win = CONV_RC + CONV_HALO
    for rc in range(ts // CONV_RC):
        for lc in range(d // LANES):
            ls = slice(lc * LANES, (lc + 1) * LANES)
            xw = ubuf[rc * CONV_RC:rc * CONV_RC + win, ls]
            acc = jnp.broadcast_to(dwb_ref[:, ls], (CONV_RC, LANES))
            for p in range(SUBLANES):
                xp = xw if p == 0 else pltpu.roll(xw, win - p, axis=0)
                for j in range(CONV_WIDTH):
                    if (off0 + j) % SUBLANES == p:
                        a0 = off0 + j - p
                        acc = acc + dw_ref[j:j + 1, ls] * xp[a0:a0 + CONV_RC, :]
            cbuf[rc * CONV_RC:(rc + 1) * CONV_RC, ls] = acc

    c = cbuf[...]
    mu = jnp.mean(c, axis=-1, keepdims=True)
    cc = c - mu
    var = jnp.mean(cc * cc, axis=-1, keepdims=True)
    n = cc * lax.rsqrt(var + NORM_EPS) * lng_ref[...] + lnb_ref[...]
    sw = (n * _sigmoid(n)).astype(BF16)
    y = jnp.dot(sw, pw2_ref[...], preferred_element_type=F32) + b2_ref[...]
    o_ref[0] = x + y


def _conformer_layer(h, nw, pw1, b1, dw, dwb, lng, lnb, pw2, b2):
    b, s, d = h.shape
    ts = CONV_ROWS
    row = lambda v: v.reshape(1, -1)
    full = lambda shape: pl.BlockSpec(shape, lambda bi, si: (0,) * len(shape))
    return pl.pallas_call(
        _conv_kernel,
        grid=(b, s // ts),
        in_specs=[
            pl.BlockSpec((1, ts, d), lambda bi, si: (bi, si, 0)),
            full((1, d)), full((d, 2 * d)), full((1, 2 * d)),
            full((CONV_WIDTH, d)), full((1, d)), full((1, d)), full((1, d)),
            full((d, d)), full((1, d)),
        ],
        out_specs=pl.BlockSpec((1, ts, d), lambda bi, si: (bi, si, 0)),
        out_shape=jax.ShapeDtypeStruct((b, s, d), F32),
        scratch_shapes=[pltpu.VMEM((ts + CONV_HALO, d), F32), pltpu.VMEM((ts, d), F32)],
        compiler_params=pltpu.CompilerParams(
            dimension_semantics=("arbitrary", "arbitrary"), vmem_limit_bytes=VMEM_LIMIT),
        name="conformer_conv",
    )(h, row(nw), pw1.astype(BF16), row(b1), dw, row(dwb), row(lng), row(lnb),
      pw2.astype(BF16), row(b2))


HGRN_GROUP = 4
GROUP_ROWS = HGRN_GROUP * CHUNK


def _hgrn_state_updates(gi, k_s, v_s, g_s, upd_s):
    rows = pl.ds(pl.multiple_of(gi * GROUP_ROWS, GROUP_ROWS), GROUP_ROWS)
    gc = g_s[rows, :]
    k = k_s[rows, :]
    chunk_of_row = lax.broadcasted_iota(jnp.int32, (GROUP_ROWS, HEAD_DIM), 0) // CHUNK
    glast = jnp.concatenate(
        [jnp.broadcast_to(gc[(j + 1) * CHUNK - 1:(j + 1) * CHUNK, :], (CHUNK, HEAD_DIM))
         for j in range(HGRN_GROUP)], axis=0)
    kd = k * jnp.exp(glast - gc)
    rhs = jnp.concatenate([jnp.where(chunk_of_row == j, kd, 0.0).astype(BF16)
                           for j in range(HGRN_GROUP)], axis=1)
    vb = v_s[rows, :].astype(BF16)
    upd = lax.dot_general(vb, rhs, (((0,), (0,)), ((), ())), preferred_element_type=F32)
    for j in range(HGRN_GROUP):
        upd_s[gi * HGRN_GROUP + j] = upd[:, j * HEAD_DIM:(j + 1) * HEAD_DIM]


def _hgrn_group(gi, q_s, k_s, v_s, g_s, o_s, sb_s):
    n = GROUP_ROWS
    rows = pl.ds(pl.multiple_of(gi * n, n), n)
    q = q_s[rows, :]
    k = k_s[rows, :]
    gc = g_s[rows, :]
    vb = v_s[rows, :].astype(BF16)
    nt = (((1,), (1,)), ((), ()))

    qg = (q * jnp.exp(gc)).astype(BF16)
    sts = sb_s[pl.ds(pl.multiple_of(gi * HGRN_GROUP * HEAD_DIM, HGRN_GROUP * HEAD_DIM),
                     HGRN_GROUP * HEAD_DIM), :]
    wide = lax.dot_general(qg, sts, nt, preferred_element_type=F32)
    inter = jnp.concatenate(
        [wide[j * CHUNK:(j + 1) * CHUNK, j * HEAD_DIM:(j + 1) * HEAD_DIM]
         for j in range(HGRN_GROUP)], axis=0)

    row = lax.broadcasted_iota(jnp.int32, (n, HEAD_DIM), 0)
    ti = lax.broadcasted_iota(jnp.int32, (n, n), 0)
    si = lax.broadcasted_iota(jnp.int32, (n, n), 1)
    tx = ti ^ si
    scores = jnp.broadcast_to(jnp.sum(q * k, axis=-1, keepdims=True), (n, n))
    end = gc
    b = 1
    while b < CHUNK:
        upper = (row & b) != 0
        e = jnp.exp(jnp.where(upper, gc - pltpu.roll(end, b, axis=0), end - gc))
        m = (jnp.where(upper, q, k) * e).astype(BF16)
        sb = lax.dot_general(m, m, nt, preferred_element_type=F32)
        scores = jnp.where(tx >= b, sb, scores)
        end = jnp.where(upper, end, pltpu.roll(end, n - b, axis=0))
        b *= 2
    scores = jnp.where((ti >= si) & (tx < CHUNK), scores, 0.0)

    intra = jnp.dot(scores.astype(BF16), vb, preferred_element_type=F32)
    o_s[rows, :] = inter + intra


def _hgrn_kernel(x_ref, nw_ref, win_ref, lb_ref, gn_ref, wout_ref, o_ref,
                 hn_s, proj_s, q_s, k_s, v_s, g_s, z_s, o_s, oh_s, upd_s, sb_s, st_ref):
    s = pl.program_id(1)
    h = pl.program_id(2)
    nh = pl.num_programs(2)
    ts = x_ref.shape[1]

    @pl.when(h == 0)
    def _():
        hn_s[...] = _rms(x_ref[0], nw_ref[...]).astype(BF16)
        proj_s[...] = jnp.dot(hn_s[...], win_ref[0], preferred_element_type=F32)

    @pl.when(s == 0)
    def _():
        st_ref[h] = jnp.zeros((HEAD_DIM, HEAD_DIM), F32)

    proj = proj_s[...]
    qr = proj[:, 0:HEAD_DIM]
    q_s[...] = qr * _sigmoid(qr)
    lb = lb_ref[0]
    forget = lb + (1.0 - lb) * _sigmoid(proj[:, HEAD_DIM:2 * HEAD_DIM])
    k_s[...] = 1.0 - forget
    v_s[...] = proj[:, 2 * HEAD_DIM:3 * HEAD_DIM]
    zr = proj[:, 3 * HEAD_DIM:4 * HEAD_DIM]
    z_s[...] = zr * _sigmoid(zr)

    g = jnp.log(forget)
    rmod = lax.broadcasted_iota(jnp.int32, (ts, 1), 0) % CHUNK
    dstep = 1
    while dstep < CHUNK:
        g = g + jnp.where(rmod >= dstep, pltpu.roll(g, dstep, axis=0), 0.0)
        dstep *= 2
    g_s[...] = g

    proj_s[...] = jnp.dot(hn_s[...], win_ref[jnp.minimum(h + 1, nh - 1)],
                          preferred_element_type=F32)

    nchunk = ts // CHUNK
    ngroup = ts // GROUP_ROWS
    for gi in range(ngroup):
        _hgrn_state_updates(gi, k_s, v_s, g_s, upd_s)

    st = st_ref[h]
    for c in range(nchunk):
        sb_s[c * HEAD_DIM:(c + 1) * HEAD_DIM, :] = st.astype(BF16)
        glast = g_s[(c + 1) * CHUNK - 1:(c + 1) * CHUNK, :]
        st = jnp.exp(glast) * st + upd_s[c]
    st_ref[h] = st

    for gi in range(ngroup):
        _hgrn_group(gi, q_s, k_s, v_s, g_s, o_s, sb_s)

    o = o_s[...]
    o = o * lax.rsqrt(jnp.mean(o * o, axis=-1, keepdims=True) + NORM_EPS) * gn_ref[...]
    oh_s[h] = (o * z_s[...]).astype(BF16)

    @pl.when(h == pl.num_programs(2) - 1)
    def _():
        oall = jnp.concatenate([oh_s[hh] for hh in range(HGRN_HEADS)], axis=1)
        o_ref[0] = x_ref[0] + jnp.dot(oall, wout_ref[...], preferred_element_type=F32)


def _hgrn_layer(h, nw, w_in, gn, w_out, lb):
    b, s, d = h.shape
    ts = HGRN_ROWS
    nh = HGRN_HEADS
    win_r = w_in.reshape(d, 4, nh, HEAD_DIM).transpose(2, 0, 1, 3).reshape(nh, d, 4 * HEAD_DIM)
    return pl.pallas_call(
        _hgrn_kernel,
        grid=(b, s // ts, nh),
        in_specs=[
            pl.BlockSpec((1, ts, d), lambda bi, si, hi: (bi, si, 0)),
            pl.BlockSpec((1, d), lambda bi, si, hi: (0, 0)),
            pl.BlockSpec((nh, d, 4 * HEAD_DIM), lambda bi, si, hi: (0, 0, 0),
                         pipeline_mode=pl.Buffered(1)),
            pl.BlockSpec((1, 1, HEAD_DIM), lambda bi, si, hi: (hi, 0, 0)),
            pl.BlockSpec((1, HEAD_DIM), lambda bi, si, hi: (0, 0)),
            pl.BlockSpec((d, d), lambda bi, si, hi: (0, 0), pipeline_mode=pl.Buffered(1)),
        ],
        out_specs=pl.BlockSpec((1, ts, d), lambda bi, si, hi: (bi, si, 0)),
        out_shape=jax.ShapeDtypeStruct((b, s, d), F32),
        scratch_shapes=[
            pltpu.VMEM((ts, d), BF16),
            pltpu.VMEM((ts, 4 * HEAD_DIM), F32),
            pltpu.VMEM((ts, HEAD_DIM), F32), pltpu.VMEM((ts, HEAD_DIM), F32),
            pltpu.VMEM((ts, HEAD_DIM), F32), pltpu.VMEM((ts, HEAD_DIM), F32),
            pltpu.VMEM((ts, HEAD_DIM), F32), pltpu.VMEM((ts, HEAD_DIM), F32),
            pltpu.VMEM((nh, ts, HEAD_DIM), BF16),
            pltpu.VMEM((ts // CHUNK, HEAD_DIM, HEAD_DIM), F32),
            pltpu.VMEM((ts // CHUNK * HEAD_DIM, HEAD_DIM), BF16),
            pltpu.VMEM((nh, HEAD_DIM, HEAD_DIM), F32),
        ],
        compiler_params=pltpu.CompilerParams(
            dimension_semantics=("arbitrary", "arbitrary", "arbitrary"),
            vmem_limit_bytes=VMEM_LIMIT),
        name="hgrn2",
    )(h, nw.reshape(1, d), win_r.astype(BF16), lb.reshape(nh, 1, HEAD_DIM),
      gn.reshape(1, HEAD_DIM), w_out.astype(BF16))


def _router_kernel(h_ref, nw_ref, wr_ref, br_ref, idx_ref, gate_ref, cnt_ref, cnt_s):
    i = pl.program_id(0)
    tm = h_ref.shape[0]

    @pl.when(i == 0)
    def _():
        cnt_s[...] = jnp.zeros_like(cnt_s)

    hn = _rms(h_ref[...], nw_ref[...])
    h1 = hn.astype(BF16)
    h2 = (hn - h1.astype(F32)).astype(BF16)
    w = wr_ref[...]
    w1 = w.astype(BF16)
    w2 = (w - w1.astype(F32)).astype(BF16)
    logits = (jnp.dot(h1, w1, preferred_element_type=F32) + jnp.dot(h1, w2, preferred_element_type=F32)
              + jnp.dot(h2, w1, preferred_element_type=F32)) + br_ref[...]
    li = lax.broadcasted_iota(jnp.int32, (tm, LANES), 1).astype(F32)
    neg = jnp.float32(-jnp.inf)
    big = jnp.float32(LANES)

    gl = jnp.where((li >= GROUP_LANE0) & (li < GROUP_LANE0 + N_GROUPS), logits, neg)
    gmax = jnp.max(gl, axis=-1, keepdims=True)
    gval = 1.0 / jnp.sum(jnp.exp(gl - gmax), axis=-1, keepdims=True)
    gidx = jnp.min(jnp.where(gl == gmax, li, big), axis=-1, keepdims=True) - GROUP_LANE0

    lo = gidx * EXPERTS_PER_GROUP
    el = jnp.where((li >= lo) & (li < lo + EXPERTS_PER_GROUP), logits, neg)
    v0 = jnp.max(el, axis=-1, keepdims=True)
    i0 = jnp.min(jnp.where(el == v0, li, big), axis=-1, keepdims=True)
    el2 = jnp.where(li == i0, neg, el)
    v1 = jnp.max(el2, axis=-1, keepdims=True)
    i1 = jnp.min(jnp.where(el2 == v1, li, big), axis=-1, keepdims=True)
    t = jnp.exp(v1 - v0)
    p0 = 1.0 / (1.0 + t)
    g0 = gval * p0
    g1 = gval * (t * p0)

    oh = jnp.where((li == i0) | (li == i1), 1.0, 0.0)
    rr = lax.broadcasted_iota(jnp.int32, (tm, tm), 0)
    cc = lax.broadcasted_iota(jnp.int32, (tm, tm), 1)
    lower = jnp.where(rr > cc, 1.0, 0.0).astype(BF16)
    before = jnp.dot(lower, oh.astype(BF16), preferred_element_type=F32) + cnt_s[...]
    r0 = jnp.sum(jnp.where(li == i0, before, 0.0), axis=-1, keepdims=True)
    r1 = jnp.sum(jnp.where(li == i1, before, 0.0), axis=-1, keepdims=True)
    cnt_s[...] = cnt_s[...] + jnp.sum(oh, axis=0, keepdims=True)

    rec = jnp.where(li == 0.0, i0, jnp.where(li == 1.0, i1,
                    jnp.where(li == 2.0, r0, jnp.where(li == 3.0, r1, 0.0))))
    idx_ref[...] = rec.T[0:8, :]
    gate_ref[...] = jnp.where(li == 0.0, g0, jnp.where(li == 1.0, g1, 0.0))
    cnt_ref[...] = cnt_s[...]


def _router(hf, nw, grp_w, grp_b, exp_w, exp_b):
    t, d = hf.shape
    tm = ROUTER_ROWS
    wr = jnp.zeros((d, LANES), F32)
    wr = wr.at[:, :N_EXPERTS].set(exp_w.reshape(d, N_EXPERTS))
    wr = wr.at[:, GROUP_LANE0:GROUP_LANE0 + N_GROUPS].set(grp_w)
    br = jnp.zeros((1, LANES), F32)
    br = br.at[0, :N_EXPERTS].set(exp_b.reshape(N_EXPERTS))
    br = br.at[0, GROUP_LANE0:GROUP_LANE0 + N_GROUPS].set(grp_b)
    return pl.pallas_call(
        _router_kernel,
        grid=(t // tm,),
        in_specs=[
            pl.BlockSpec((tm, d), lambda i: (i, 0)),
            pl.BlockSpec((1, d), lambda i: (0, 0)),
            pl.BlockSpec((d, LANES), lambda i: (0, 0)),
            pl.BlockSpec((1, LANES), lambda i: (0, 0)),
        ],
        out_specs=[
            pl.BlockSpec((8, tm), lambda i: (0, i)),
            pl.BlockSpec((tm, LANES), lambda i: (i, 0)),
            pl.BlockSpec((1, LANES), lambda i: (0, 0)),
        ],
        out_shape=[
            jax.ShapeDtypeStruct((8, t), F32),
            jax.ShapeDtypeStruct((t, LANES), F32),
            jax.ShapeDtypeStruct((1, LANES), F32),
        ],
        scratch_shapes=[pltpu.VMEM((1, LANES), F32)],
        compiler_params=pltpu.CompilerParams(
            dimension_semantics=("arbitrary",), vmem_limit_bytes=VMEM_LIMIT),
        name="moe_router",
    )(hf, nw.reshape(1, d), wr, br)


def _dispatch_kernel(d0_ref, d1_ref, zf_ref, h_ref, nw_ref, xs_hbm, stage, zbuf, sem, zsem):
    i = pl.program_id(0)
    nb = pl.num_programs(0)
    tm = h_ref.shape[0]
    blk = zbuf.shape[0]
    slot = i % 2

    def wait_slot(s):
        for _ in range(2):
            pltpu.make_async_copy(xs_hbm.at[pl.ds(0, tm)], xs_hbm.at[pl.ds(0, tm)], sem.at[s]).wait()

    @pl.when(i == 0)
    def _():
        zbuf[...] = jnp.zeros_like(zbuf)

        def zbody(e, carry):
            @pl.when(zf_ref[e] >= 0)
            def _():
                start = pl.multiple_of(zf_ref[e], blk)
                cp = pltpu.make_async_copy(zbuf, xs_hbm.at[pl.ds(start, blk)], zsem)
                cp.start()
                cp.wait()
            return carry
        lax.fori_loop(0, N_EXPERTS, zbody, 0)

        def tbody(b, carry):
            cp = pltpu.make_async_copy(zbuf, xs_hbm.at[pl.ds(pl.multiple_of(b * blk, blk), blk)], zsem)
            cp.start()
            cp.wait()
            return carry
        lax.fori_loop(zf_ref[N_EXPERTS], xs_hbm.shape[0] // blk, tbody, 0)

    @pl.when(i >= 2)
    def _():
        wait_slot(slot)

    stage[slot] = _rms(h_ref[...], nw_ref[...]).reshape(tm // SUBLANES, SUBLANES, h_ref.shape[1])

    def body(rr, carry):
        for j in range(DMA_UNROLL):
            t = i * tm + rr * DMA_UNROLL + j
            src = stage.at[slot, rr, pl.ds(j, 1)]
            for d_ref in (d0_ref, d1_ref):
                dst = xs_hbm.at[pl.ds(d_ref[t], 1)]
                pltpu.make_async_copy(src, dst, sem.at[slot]).start()
        return carry
    lax.fori_loop(0, tm // DMA_UNROLL, body, 0)

    @pl.when(i == nb - 1)
    def _():
        wait_slot(slot)

        @pl.when(nb >= 2)
        def _():
            wait_slot(1 - slot)


def _dispatch(hf, nw, dest0, dest1, zfill, n_rows):
    t, d = hf.shape
    tm = DISPATCH_ROWS
    grid_spec = pltpu.PrefetchScalarGridSpec(
        num_scalar_prefetch=3,
        grid=(t // tm,),
        in_specs=[
            pl.BlockSpec((tm, d), lambda i, a, b, c: (i, 0)),
            pl.BlockSpec((1, d), lambda i, a, b, c: (0, 0)),
        ],
        out_specs=pl.BlockSpec(memory_space=pl.ANY),
        scratch_shapes=[pltpu.VMEM((2, tm // SUBLANES, SUBLANES, d), F32),
                        pltpu.VMEM((MOE_BLOCK, d), F32),
                        pltpu.SemaphoreType.DMA((2,)), pltpu.SemaphoreType.DMA(())],
    )
    return pl.pallas_call(
        _dispatch_kernel,
        grid_spec=grid_spec,
        out_shape=jax.ShapeDtypeStruct((n_rows, d), F32),
        compiler_params=pltpu.CompilerParams(
            dimension_semantics=("arbitrary",), vmem_limit_bytes=VMEM_LIMIT),
        name="moe_dispatch",
    )(dest0, dest1, zfill, hf, nw.reshape(1, d))


def _expert_kernel(be_ref, nu_ref, x_ref, wg_ref, wu_ref, wd_ref, y_ref, wg_s, wu_s, wd_s):
    i = pl.program_id(0)
    prev = be_ref[jnp.maximum(i - 1, 0)]

    @pl.when(i < nu_ref[0])
    def _():
        @pl.when((i == 0) | (be_ref[i] != prev))
        def _():
            wg_s[...] = wg_ref[0, 0].astype(BF16)
            wu_s[...] = wu_ref[0, 0].astype(BF16)
            wd_s[...] = wd_ref[0, 0].astype(BF16)

        hn = x_ref[...].astype(BF16)
        g = jnp.dot(hn, wg_s[...], preferred_element_type=F32)
        u = jnp.dot(hn, wu_s[...], preferred_element_type=F32)
        a = (g * _sigmoid(g) * u).astype(BF16)
        y_ref[...] = jnp.dot(a, wd_s[...], preferred_element_type=F32)

    @pl.when(i >= nu_ref[0])
    def _():
        y_ref[...] = jnp.zeros_like(y_ref)


def _experts(xs, block_expert, n_used, w_gate, w_up, w_down, layer):
    n_rows = xs.shape[0]
    blk = MOE_BLOCK
    d, de = w_gate.shape[-2:]
    used_blk = lambda i, be, nu: (jnp.minimum(i, nu[0] - 1), 0)
    grid_spec = pltpu.PrefetchScalarGridSpec(
        num_scalar_prefetch=2,
        grid=(n_rows // blk,),
        in_specs=[
            pl.BlockSpec((blk, d), used_blk),
            pl.BlockSpec((1, 1, d, de), lambda i, be, nu: (layer, be[i], 0, 0)),
            pl.BlockSpec((1, 1, d, de), lambda i, be, nu: (layer, be[i], 0, 0)),
            pl.BlockSpec((1, 1, de, d), lambda i, be, nu: (layer, be[i], 0, 0)),
        ],
        out_specs=pl.BlockSpec((blk, d), lambda i, be, nu: (i, 0)),
        scratch_shapes=[pltpu.VMEM((d, de), BF16), pltpu.VMEM((d, de), BF16),
                        pltpu.VMEM((de, d), BF16)],
    )
    return pl.pallas_call(
        _expert_kernel,
        grid_spec=grid_spec,
        out_shape=jax.ShapeDtypeStruct((n_rows, d), F32),
        compiler_params=pltpu.CompilerParams(
            dimension_semantics=("arbitrary",), vmem_limit_bytes=VMEM_LIMIT),
        name="moe_experts",
    )(block_expert, n_used, xs, w_gate, w_up, w_down)


def _combine_kernel(d0_ref, d1_ref, y_hbm, h_ref, gate_ref, fw_ref, o_ref, ybuf, sem, *, final_norm):
    i = pl.program_id(0)
    nb = pl.num_programs(0)
    tc = h_ref.shape[0]

    def issue(b, slot):
        def body(rr, carry):
            for j in range(DMA_UNROLL):
                t = b * tc + rr * DMA_UNROLL + j
                for kk, d_ref in enumerate((d0_ref, d1_ref)):
                    pltpu.make_async_copy(y_hbm.at[pl.ds(d_ref[t], 1)],
                                          ybuf.at[slot, kk, rr, pl.ds(j, 1)], sem.at[slot]).start()
            return carry
        lax.fori_loop(0, tc // DMA_UNROLL, body, 0)

    @pl.when(i == 0)
    def _():
        issue(0, 0)

    @pl.when(i + 1 < nb)
    def _():
        issue(i + 1, (i + 1) % 2)

    slot = i % 2
    for kk in range(2):
        pltpu.make_async_copy(y_hbm.at[pl.ds(0, tc)], y_hbm.at[pl.ds(0, tc)], sem.at[slot]).wait()

    gate = gate_ref[...]
    y0 = ybuf[slot, 0].reshape(tc, h_ref.shape[1])
    y1 = ybuf[slot, 1].reshape(tc, h_ref.shape[1])
    o = h_ref[...] + (gate[:, 0:1] * y0 + gate[:, 1:2] * y1)
    if final_norm:
        o = _rms(o, fw_ref[...])
    o_ref[...] = o


def _combine(hf, y, dest0, dest1, gates, fw, final_norm):
    t, d = hf.shape
    tc = COMBINE_ROWS
    grid_spec = pltpu.PrefetchScalarGridSpec(
        num_scalar_prefetch=2,
        grid=(t // tc,),
        in_specs=[
            pl.BlockSpec(memory_space=pl.ANY),
            pl.BlockSpec((tc, d), lambda i, a, b: (i, 0)),
            pl.BlockSpec((tc, LANES), lambda i, a, b: (i, 0)),
            pl.BlockSpec((1, d), lambda i, a, b: (0, 0)),
        ],
        out_specs=pl.BlockSpec((tc, d), lambda i, a, b: (i, 0)),
        scratch_shapes=[pltpu.VMEM((2, 2, tc // SUBLANES, SUBLANES, d), F32),
                        pltpu.SemaphoreType.DMA((2,))],
    )
    return pl.pallas_call(
        functools.partial(_combine_kernel, final_norm=final_norm),
        grid_spec=grid_spec,
        out_shape=jax.ShapeDtypeStruct((t, d), F32),
        compiler_params=pltpu.CompilerParams(
            dimension_semantics=("arbitrary",), vmem_limit_bytes=VMEM_LIMIT),
        name="moe_combine",
    )(dest0, dest1, y, hf, gates, fw.reshape(1, d))


def _moe_layer(h, nw, grp_w, grp_b, exp_w, exp_b, w_gate, w_up, w_down, fw, layer, final_norm):
    b, s, d = h.shape
    t = b * s
    hf = h.reshape(t, d)
    rec, gates, cnt = _router(hf, nw, grp_w, grp_b, exp_w, exp_b)

    rec = rec.astype(jnp.int32)
    counts = cnt[0, :N_EXPERTS].astype(jnp.int32)
    padded = (counts + MOE_BLOCK - 1) // MOE_BLOCK * MOE_BLOCK
    pad_end = jnp.cumsum(padded)
    pad_start = pad_end - padded
    eids = jnp.arange(N_EXPERTS, dtype=jnp.int32)[:, None]
    dest0 = jnp.sum(jnp.where(rec[0][None, :] == eids, pad_start[:, None], 0), axis=0) + rec[2]
    dest1 = jnp.sum(jnp.where(rec[1][None, :] == eids, pad_start[:, None], 0), axis=0) + rec[3]
    n_rows = 2 * t + N_EXPERTS * MOE_BLOCK
    n_blocks = n_rows // MOE_BLOCK
    block_start = jnp.arange(n_blocks, dtype=jnp.int32) * MOE_BLOCK
    block_expert = jnp.minimum(
        jnp.sum((block_start[:, None] >= pad_end[None, :]).astype(jnp.int32), axis=1), N_EXPERTS - 1)
    n_used = (pad_end[-1:] // MOE_BLOCK).astype(jnp.int32)
    zfill = jnp.concatenate([jnp.where(padded > 0, pad_end - MOE_BLOCK, -1).astype(jnp.int32), n_used])

    xs = _dispatch(hf, nw, dest0, dest1, zfill, n_rows)
    y = _experts(xs, block_expert, n_used, w_gate, w_up, w_down, layer)
    out = _combine(hf, y, dest0, dest1, gates, fw, final_norm)
    return out.reshape(b, s, d)


def kernel(x, conv_norm_w, conv_pw1_w, conv_pw1_b, conv_dw_w, conv_dw_b, conv_ln_g, conv_ln_b,
           conv_pw2_w, conv_pw2_b, hgrn_norm_w, hgrn_w_in, hgrn_gnorm_w, hgrn_w_out, lower_bounds,
           ffn_norm_w, router_grp_w, router_grp_b, router_exp_w, router_exp_b,
           moe_w_gate, moe_w_up, moe_w_down, final_norm_w):
    depth = lower_bounds.shape[0]
    lb_p = jax.nn.softmax(lower_bounds.astype(F32), axis=0)
    lb_all = jnp.cumsum(lb_p, axis=0) - lb_p[0]
    h = x
    for layer in range(depth):
        j = layer // 2
        if layer % 2 == 0:
            h = _conformer_layer(h, conv_norm_w[j], conv_pw1_w[j], conv_pw1_b[j], conv_dw_w[j],
                                 conv_dw_b[j], conv_ln_g[j], conv_ln_b[j], conv_pw2_w[j],
                                 conv_pw2_b[j])
        else:
            h = _hgrn_layer(h, hgrn_norm_w[j], hgrn_w_in[j], hgrn_gnorm_w[j], hgrn_w_out[j],
                            lb_all[layer])
        h = _moe_layer(h, ffn_norm_w[layer], router_grp_w[layer], router_grp_b[layer],
                       router_exp_w[layer], router_exp_b[layer], moe_w_gate, moe_w_up,
                       moe_w_down, final_norm_w, layer=layer, final_norm=(layer == depth - 1))
    return h
```

```python
import functools

import jax
import jax.numpy as jnp
from jax import lax
from jax.experimental import pallas as pl
from jax.experimental.pallas import tpu as pltpu

F32 = jnp.float32
BF16 = jnp.bfloat16

NORM_EPS = 1e-6
CONV_WIDTH = 31
CONV_HALO = 32
HGRN_HEADS = 8
HEAD_DIM = 128
CHUNK = 64
N_GROUPS = 4
EXPERTS_PER_GROUP = 8
N_EXPERTS = N_GROUPS * EXPERTS_PER_GROUP
LANES = 128
SUBLANES = 8
GROUP_LANE0 = N_EXPERTS

CONV_ROWS = 256
CONV_RC = 128
HGRN_ROWS = 1024
ROUTER_ROWS = 512
MOE_BLOCK = 512
DMA_UNROLL = 8
DISPATCH_ROWS = 512
COMBINE_ROWS = 512
VMEM_LIMIT = 48 * 1024 * 1024


def _rms(x, w):
    ms = jnp.mean(x * x, axis=-1, keepdims=True)
    return x * lax.rsqrt(ms + NORM_EPS) * w


def _sigmoid(x):
    return 1.0 / (1.0 + jnp.exp(-x))


def _conv_kernel(x_ref, nw_ref, pw1_ref, b1_ref, dw_ref, dwb_ref, lng_ref, lnb_ref,
                 pw2_ref, b2_ref, o_ref, ubuf, cbuf):
    s = pl.program_id(1)
    ts = x_ref.shape[1]
    d = x_ref.shape[2]
    x = x_ref[0]
    hn = _rms(x, nw_ref[...]).astype(BF16)
    a = jnp.dot(hn, pw1_ref[...], preferred_element_type=F32) + b1_ref[...]
    u = a[:, :d] * _sigmoid(a[:, d:])

    @pl.when(s == 0)
    def _():
        ubuf[0:CONV_HALO, :] = jnp.zeros((CONV_HALO, d), F32)

    @pl.when(s > 0)
    def _():
        ubuf[0:CONV_HALO, :] = ubuf[ts:ts + CONV_HALO, :]

    ubuf[CONV_HALO:, :] = u

    off0 = CONV_HALO - (CONV_WIDTH - 1)
    win = CONV_RC + CONV_HALO
    for rc in range(ts // CONV_RC):
        for lc in range(d // LANES):
            ls = slice(lc * LANES, (lc + 1) * LANES)
            xw = ubuf[rc * CONV_RC:rc * CONV_RC + win, ls]
            acc = jnp.broadcast_to(dwb_ref[:, ls], (CONV_RC, LANES))
            for p in range(SUBLANES):
                xp = xw if p == 0 else pltpu.roll(xw, win - p, axis=0)
                for j in range(CONV_WIDTH):
                    if (off0 + j) % SUBLANES == p:
                        a0 = off0 + j - p
                        acc = acc + dw_ref[j:j + 1, ls] * xp[a0:a0 + CONV_RC, :]
            cbuf[rc * CONV_RC:(rc + 1) * CONV_RC, ls] = acc

    c = cbuf[...]
    mu = jnp.mean(c, axis=-1, keepdims=True)
    cc = c - mu
    var = jnp.mean(cc * cc, axis=-1, keepdims=True)
    n = cc * lax.rsqrt(var + NORM_EPS) * lng_ref[...] + lnb_ref[...]
    sw = (n * _sigmoid(n)).astype(BF16)
    y = jnp.dot(sw, pw2_ref[...], preferred_element_type=F32) + b2_ref[...]
    o_ref[0] = x + y


def _conformer_layer(h, nw, pw1, b1, dw, dwb, lng, lnb, pw2, b2):
    b, s, d = h.shape
    ts = CONV_ROWS
    row = lambda v: v.reshape(1, -1)
    full = lambda shape: pl.BlockSpec(shape, lambda bi, si: (0,) * len(shape))
    return pl.pallas_call(
        _conv_kernel,
        grid=(b, s // ts),
        in_specs=[
            pl.BlockSpec((1, ts, d), lambda bi, si: (bi, si, 0)),
            full((1, d)), full((d, 2 * d)), full((1, 2 * d)),
            full((CONV_WIDTH, d)), full((1, d)), full((1, d)), full((1, d)),
            full((d, d)), full((1, d)),
        ],
        out_specs=pl.BlockSpec((1, ts, d), lambda bi, si: (bi, si, 0)),
        out_shape=jax.ShapeDtypeStruct((b, s, d), F32),
        scratch_shapes=[pltpu.VMEM((ts + CONV_HALO, d), F32), pltpu.VMEM((ts, d), F32)],
        compiler_params=pltpu.CompilerParams(
            dimension_semantics=("arbitrary", "arbitrary"), vmem_limit_bytes=VMEM_LIMIT),
        name="conformer_conv",
    )(h, row(nw), pw1.astype(BF16), row(b1), dw, row(dwb), row(lng), row(lnb),
      pw2.astype(BF16), row(b2))


HGRN_GROUP = 4
GROUP_ROWS = HGRN_GROUP * CHUNK


def _hgrn_state_updates(gi, k_s, v_s, g_s, upd_s):
    rows = pl.ds(pl.multiple_of(gi * GROUP_ROWS, GROUP_ROWS), GROUP_ROWS)
    gc = g_s[rows, :]
    k = k_s[rows, :]
    chunk_of_row = lax.broadcasted_iota(jnp.int32, (GROUP_ROWS, HEAD_DIM), 0) // CHUNK
    glast = jnp.concatenate(
        [jnp.broadcast_to(gc[(j + 1) * CHUNK - 1:(j + 1) * CHUNK, :], (CHUNK, HEAD_DIM))
         for j in range(HGRN_GROUP)], axis=0)
    kd = k * jnp.exp(glast - gc)
    rhs = jnp.concatenate([jnp.where(chunk_of_row == j, kd, 0.0).astype(BF16)
                           for j in range(HGRN_GROUP)], axis=1)
    vb = v_s[rows, :].astype(BF16)
    upd = lax.dot_general(vb, rhs, (((0,), (0,)), ((), ())), preferred_element_type=F32)
    for j in range(HGRN_GROUP):
        upd_s[gi * HGRN_GROUP + j] = upd[:, j * HEAD_DIM:(j + 1) * HEAD_DIM]


def _hgrn_group(gi, q_s, k_s, v_s, g_s, o_s, sb_s):
    n = GROUP_ROWS
    rows = pl.ds(pl.multiple_of(gi * n, n), n)
    q = q_s[rows, :]
    k = k_s[rows, :]
    gc = g_s[rows, :]
    vb = v_s[rows, :].astype(BF16)
    nt = (((1,), (1,)), ((), ()))

    qg = (q * jnp.exp(gc)).astype(BF16)
    sts = sb_s[pl.ds(pl.multiple_of(gi * HGRN_GROUP * HEAD_DIM, HGRN_GROUP * HEAD_DIM),
                     HGRN_GROUP * HEAD_DIM), :]
    wide = lax.dot_general(qg, sts, nt, preferred_element_type=F32)
    inter = jnp.concatenate(
        [wide[j * CHUNK:(j + 1) * CHUNK, j * HEAD_DIM:(j + 1) * HEAD_DIM]
         for j in range(HGRN_GROUP)], axis=0)

    row = lax.broadcasted_iota(jnp.int32, (n, HEAD_DIM), 0)
    ti = lax.broadcasted_iota(jnp.int32, (n, n), 0)
    si = lax.broadcasted_iota(jnp.int32, (n, n), 1)
    tx = ti ^ si
    scores = jnp.broadcast_to(jnp.sum(q * k, axis=-1, keepdims=True), (n, n))
    end = gc
    b = 1
    while b < CHUNK:
        upper = (row & b) != 0
        e = jnp.exp(jnp.where(upper, gc - pltpu.roll(end, b, axis=0), end - gc))
        m = (jnp.where(upper, q, k) * e).astype(BF16)
        sb = lax.dot_general(m, m, nt, preferred_element_type=F32)
        scores = jnp.where(tx >= b, sb, scores)
        end = jnp.where(upper, end, pltpu.roll(end, n - b, axis=0))
        b *= 2
    scores = jnp.where((ti >= si) & (tx < CHUNK), scores, 0.0)

    intra = jnp.dot(scores.astype(BF16), vb, preferred_element_type=F32)
    o_s[rows, :] = inter + intra


def _hgrn_kernel(x_ref, nw_ref, win_ref, lb_ref, gn_ref, wout_ref, o_ref,
                 hn_s, proj_s, q_s, k_s, v_s, g_s, z_s, o_s, oh_s, upd_s, sb_s, st_ref):
    s = pl.program_id(1)
    h = pl.program_id(2)
    nh = pl.num_programs(2)
    ts = x_ref.shape[1]

    @pl.when(h == 0)
    def _():
        hn_s[...] = _rms(x_ref[0], nw_ref[...]).astype(BF16)
        proj_s[...] = jnp.dot(hn_s[...], win_ref[0], preferred_element_type=F32)

    @pl.when(s == 0)
    def _():
        st_ref[h] = jnp.zeros((HEAD_DIM, HEAD_DIM), F32)

    proj = proj_s[...]
    qr = proj[:, 0:HEAD_DIM]
    q_s[...] = qr * _sigmoid(qr)
    lb = lb_ref[0]
    forget = lb + (1.0 - lb) * _sigmoid(proj[:, HEAD_DIM:2 * HEAD_DIM])
    k_s[...] = 1.0 - forget
    v_s[...] = proj[:, 2 * HEAD_DIM:3 * HEAD_DIM]
    zr = proj[:, 3 * HEAD_DIM:4 * HEAD_DIM]
    z_s[...] = zr * _sigmoid(zr)

    g = jnp.log(forget)
    rmod = lax.broadcasted_iota(jnp.int32, (ts, 1), 0) % CHUNK
    dstep = 1
    while dstep < CHUNK:
        g = g + jnp.where(rmod >= dstep, pltpu.roll(g, dstep, axis=0), 0.0)
        dstep *= 2
    g_s[...] = g

    proj_s[...] = jnp.dot(hn_s[...], win_ref[jnp.minimum(h + 1, nh - 1)],
                          preferred_element_type=F32)

    nchunk = ts // CHUNK
    ngroup = ts // GROUP_ROWS
    for gi in range(ngroup):
        _hgrn_state_updates(gi, k_s, v_s, g_s, upd_s)

    st = st_ref[h]
    for c in range(nchunk):
        sb_s[c * HEAD_DIM:(c + 1) * HEAD_DIM, :] = st.astype(BF16)
        glast = g_s[(c + 1) * CHUNK - 1:(c + 1) * CHUNK, :]
        st = jnp.exp(glast) * st + upd_s[c]
    st_ref[h] = st

    for gi in range(ngroup):
        _hgrn_group(gi, q_s, k_s, v_s, g_s, o_s, sb_s)

    o = o_s[...]
    o = o * lax.rsqrt(jnp.mean(o * o, axis=-1, keepdims=True) + NORM_EPS) * gn_ref[...]
    oh_s[h] = (o * z_s[...]).astype(BF16)

    @pl.when(h == pl.num_programs(2) - 1)
    def _():
        oall = jnp.concatenate([oh_s[hh] for hh in range(HGRN_HEADS)], axis=1)
        o_ref[0] = x_ref[0] + jnp.dot(oall, wout_ref[...], preferred_element_type=F32)


def _hgrn_layer(h, nw, w_in, gn, w_out, lb):
    b, s, d = h.shape
    ts = HGRN_ROWS
    nh = HGRN_HEADS
    win_r = w_in.reshape(d, 4, nh, HEAD_DIM).transpose(2, 0, 1, 3).reshape(nh, d, 4 * HEAD_DIM)
    return pl.pallas_call(
        _hgrn_kernel,
        grid=(b, s // ts, nh),
        in_specs=[
            pl.BlockSpec((1, ts, d), lambda bi, si, hi: (bi, si, 0)),
            pl.BlockSpec((1, d), lambda bi, si, hi: (0, 0)),
            pl.BlockSpec((nh, d, 4 * HEAD_DIM), lambda bi, si, hi: (0, 0, 0),
                         pipeline_mode=pl.Buffered(1)),
            pl.BlockSpec((1, 1, HEAD_DIM), lambda bi, si, hi: (hi, 0, 0)),
            pl.BlockSpec((1, HEAD_DIM), lambda bi, si, hi: (0, 0)),
            pl.BlockSpec((d, d), lambda bi, si, hi: (0, 0), pipeline_mode=pl.Buffered(1)),
        ],
        out_specs=pl.BlockSpec((1, ts, d), lambda bi, si, hi: (bi, si, 0)),
        out_shape=jax.ShapeDtypeStruct((b, s, d), F32),
        scratch_shapes=[
            pltpu.VMEM((ts, d), BF16),
            pltpu.VMEM((ts, 4 * HEAD_DIM), F32),
            pltpu.VMEM((ts, HEAD_DIM), F32), pltpu.VMEM((ts, HEAD_DIM), F32),
            pltpu.VMEM((ts, HEAD_DIM), F32), pltpu.VMEM((ts, HEAD_DIM), F32),
            pltpu.VMEM((ts, HEAD_DIM), F32), pltpu.VMEM((ts, HEAD_DIM), F32),
            pltpu.VMEM((nh, ts, HEAD_DIM), BF16),
            pltpu.VMEM((ts // CHUNK, HEAD_DIM, HEAD_DIM), F32),
            pltpu.VMEM((ts // CHUNK * HEAD_DIM, HEAD_DIM), BF16),
            pltpu.VMEM((nh, HEAD_DIM, HEAD_DIM), F32),
        ],
        compiler_params=pltpu.CompilerParams(
            dimension_semantics=("arbitrary", "arbitrary", "arbitrary"),
            vmem_limit_bytes=VMEM_LIMIT),
        name="hgrn2",
    )(h, nw.reshape(1, d), win_r.astype(BF16), lb.reshape(nh, 1, HEAD_DIM),
      gn.reshape(1, HEAD_DIM), w_out.astype(BF16))


def _router_kernel(h_ref, nw_ref, wr_ref, br_ref, idx_ref, gate_ref, cnt_ref, cnt_s):
    i = pl.program_id(0)
    tm = h_ref.shape[0]

    @pl.when(i == 0)
    def _():
        cnt_s[...] = jnp.zeros_like(cnt_s)

    hn = _rms(h_ref[...], nw_ref[...])
    h1 = hn.astype(BF16)
    h2 = (hn - h1.astype(F32)).astype(BF16)
    w = wr_ref[...]
    w1 = w.astype(BF16)
    w2 = (w - w1.astype(F32)).astype(BF16)
    logits = (jnp.dot(h1, w1, preferred_element_type=F32) + jnp.dot(h1, w2, preferred_element_type=F32)
              + jnp.dot(h2, w1, preferred_element_type=F32)) + br_ref[...]
    li = lax.broadcasted_iota(jnp.int32, (tm, LANES), 1).astype(F32)
    neg = jnp.float32(-jnp.inf)
    big = jnp.float32(LANES)

    gl = jnp.where((li >= GROUP_LANE0) & (li < GROUP_LANE0 + N_GROUPS), logits, neg)
    gmax = jnp.max(gl, axis=-1, keepdims=True)
    gval = 1.0 / jnp.sum(jnp.exp(gl - gmax), axis=-1, keepdims=True)
    gidx = jnp.min(jnp.where(gl == gmax, li, big), axis=-1, keepdims=True) - GROUP_LANE0

    lo = gidx * EXPERTS_PER_GROUP
    el = jnp.where((li >= lo) & (li < lo + EXPERTS_PER_GROUP), logits, neg)
    v0 = jnp.max(el, axis=-1, keepdims=True)
    i0 = jnp.min(jnp.where(el == v0, li, big), axis=-1, keepdims=True)
    el2 = jnp.where(li == i0, neg, el)
    v1 = jnp.max(el2, axis=-1, keepdims=True)
    i1 = jnp.min(jnp.where(el2 == v1, li, big), axis=-1, keepdims=True)
    t = jnp.exp(v1 - v0)
    p0 = 1.0 / (1.0 + t)
    g0 = gval * p0
    g1 = gval * (t * p0)

    oh = jnp.where((li == i0) | (li == i1), 1.0, 0.0)
    rr = lax.broadcasted_iota(jnp.int32, (tm, tm), 0)
    cc = lax.broadcasted_iota(jnp.int32, (tm, tm), 1)
    lower = jnp.where(rr > cc, 1.0, 0.0).astype(BF16)
    before = jnp.dot(lower, oh.astype(BF16), preferred_element_type=F32) + cnt_s[...]
    r0 = jnp.sum(jnp.where(li == i0, before, 0.0), axis=-1, keepdims=True)
    r1 = jnp.sum(jnp.where(li == i1, before, 0.0), axis=-1, keepdims=True)
    cnt_s[...] = cnt_s[...] + jnp.sum(oh, axis=0, keepdims=True)

    rec = jnp.where(li == 0.0, i0, jnp.where(li == 1.0, i1,
                    jnp.where(li == 2.0, r0, jnp.where(li == 3.0, r1, 0.0))))
    idx_ref[...] = rec.T[0:8, :]
    gate_ref[...] = jnp.where(li == 0.0, g0, jnp.where(li == 1.0, g1, 0.0))
    cnt_ref[...] = cnt_s[...]


def _router(hf, nw, grp_w, grp_b, exp_w, exp_b):
    t, d = hf.shape
    tm = ROUTER_ROWS
    wr = jnp.zeros((d, LANES), F32)
    wr = wr.at[:, :N_EXPERTS].set(exp_w.reshape(d, N_EXPERTS))
    wr = wr.at[:, GROUP_LANE0:GROUP_LANE0 + N_GROUPS].set(grp_w)
    br = jnp.zeros((1, LANES), F32)
    br = br.at[0, :N_EXPERTS].set(exp_b.reshape(N_EXPERTS))
    br = br.at[0, GROUP_LANE0:GROUP_LANE0 + N_GROUPS].set(grp_b)
    return pl.pallas_call(
        _router_kernel,
        grid=(t // tm,),
        in_specs=[
            pl.BlockSpec((tm, d), lambda i: (i, 0)),
            pl.BlockSpec((1, d), lambda i: (0, 0)),
            pl.BlockSpec((d, LANES), lambda i: (0, 0)),
            pl.BlockSpec((1, LANES), lambda i: (0, 0)),
        ],
        out_specs=[
            pl.BlockSpec((8, tm), lambda i: (0, i)),
            pl.BlockSpec((tm, LANES), lambda i: (i, 0)),
            pl.BlockSpec((1, LANES), lambda i: (0, 0)),
        ],
        out_shape=[
            jax.ShapeDtypeStruct((8, t), F32),
            jax.ShapeDtypeStruct((t, LANES), F32),
            jax.ShapeDtypeStruct((1, LANES), F32),
        ],
        scratch_shapes=[pltpu.VMEM((1, LANES), F32)],
        compiler_params=pltpu.CompilerParams(
            dimension_semantics=("arbitrary",), vmem_limit_bytes=VMEM_LIMIT),
        name="moe_router",
    )(hf, nw.reshape(1, d), wr, br)


def _dispatch_kernel(d0_ref, d1_ref, zf_ref, h_ref, nw_ref, xs_hbm, stage, zbuf, sem, zsem):
    i = pl.program_id(0)
    nb = pl.num_programs(0)
    tm = h_ref.shape[0]
    blk = zbuf.shape[0]
    slot = i % 2

    def wait_slot(s):
        for _ in range(2):
            pltpu.make_async_copy(xs_hbm.at[pl.ds(0, tm)], xs_hbm.at[pl.ds(0, tm)], sem.at[s]).wait()

    @pl.when(i == 0)
    def _():
        zbuf[...] = jnp.zeros_like(zbuf)

        def zbody(e, carry):
            @pl.when(zf_ref[e] >= 0)
            def _():
                start = pl.multiple_of(zf_ref[e], blk)
                cp = pltpu.make_async_copy(zbuf, xs_hbm.at[pl.ds(start, blk)], zsem)
                cp.start()
                cp.wait()
            return carry
        lax.fori_loop(0, N_EXPERTS, zbody, 0)

        def tbody(b, carry):
            cp = pltpu.make_async_copy(zbuf, xs_hbm.at[pl.ds(pl.multiple_of(b * blk, blk), blk)], zsem)
            cp.start()
            cp.wait()
            return carry
        lax.fori_loop(zf_ref[N_EXPERTS], xs_hbm.shape[0] // blk, tbody, 0)

    @pl.when(i >= 2)
    def _():
        wait_slot(slot)

    stage[slot] = _rms(h_ref[...], nw_ref[...]).reshape(tm // SUBLANES, SUBLANES, h_ref.shape[1])

    def body(rr, carry):
        for j in range(DMA_UNROLL):
            t = i * tm + rr * DMA_UNROLL + j
            src = stage.at[slot, rr, pl.ds(j, 1)]
            for kk, d_ref in enumerate((d0_ref, d1_ref)):
                dst = xs_hbm.at[pl.ds(d_ref[t], 1)]
                pltpu.make_async_copy(src, dst, sem.at[slot]).start(priority=kk)
        return carry
    lax.fori_loop(0, tm // DMA_UNROLL, body, 0)

    @pl.when(i == nb - 1)
    def _():
        wait_slot(slot)

        @pl.when(nb >= 2)
        def _():
            wait_slot(1 - slot)


def _dispatch(hf, nw, dest0, dest1, zfill, n_rows):
    t, d = hf.shape
    tm = DISPATCH_ROWS
    grid_spec = pltpu.PrefetchScalarGridSpec(
        num_scalar_prefetch=3,
        grid=(t // tm,),
        in_specs=[
            pl.BlockSpec((tm, d), lambda i, a, b, c: (i, 0)),
            pl.BlockSpec((1, d), lambda i, a, b, c: (0, 0)),
        ],
        out_specs=pl.BlockSpec(memory_space=pl.ANY),
        scratch_shapes=[pltpu.VMEM((2, tm // SUBLANES, SUBLANES, d), F32),
                        pltpu.VMEM((MOE_BLOCK, d), F32),
                        pltpu.SemaphoreType.DMA((2,)), pltpu.SemaphoreType.DMA(())],
    )
    return pl.pallas_call(
        _dispatch_kernel,
        grid_spec=grid_spec,
        out_shape=jax.ShapeDtypeStruct((n_rows, d), F32),
        compiler_params=pltpu.CompilerParams(
            dimension_semantics=("arbitrary",), vmem_limit_bytes=VMEM_LIMIT),
        name="moe_dispatch",
    )(dest0, dest1, zfill, hf, nw.reshape(1, d))


def _expert_kernel(be_ref, nu_ref, x_ref, wg_ref, wu_ref, wd_ref, y_ref, wg_s, wu_s, wd_s):
    i = pl.program_id(0)
    prev = be_ref[jnp.maximum(i - 1, 0)]

    @pl.when(i < nu_ref[0])
    def _():
        @pl.when((i == 0) | (be_ref[i] != prev))
        def _():
            wg_s[...] = wg_ref[0, 0].astype(BF16)
            wu_s[...] = wu_ref[0, 0].astype(BF16)
            wd_s[...] = wd_ref[0, 0].astype(BF16)

        hn = x_ref[...].astype(BF16)
        g = jnp.dot(hn, wg_s[...], preferred_element_type=F32)
        u = jnp.dot(hn, wu_s[...], preferred_element_type=F32)
        a = (g * _sigmoid(g) * u).astype(BF16)
        y_ref[...] = jnp.dot(a, wd_s[...], preferred_element_type=F32)

    @pl.when(i >= nu_ref[0])
    def _():
        y_ref[...] = jnp.zeros_like(y_ref)


def _experts(xs, block_expert, n_used, w_gate, w_up, w_down, layer):
    n_rows = xs.shape[0]
    blk = MOE_BLOCK
    d, de = w_gate.shape[-2:]
    used_blk = lambda i, be, nu: (jnp.minimum(i, nu[0] - 1), 0)
    grid_spec = pltpu.PrefetchScalarGridSpec(
        num_scalar_prefetch=2,
        grid=(n_rows // blk,),
        in_specs=[
            pl.BlockSpec((blk, d), used_blk),
            pl.BlockSpec((1, 1, d, de), lambda i, be, nu: (layer, be[i], 0, 0)),
            pl.BlockSpec((1, 1, d, de), lambda i, be, nu: (layer, be[i], 0, 0)),
            pl.BlockSpec((1, 1, de, d), lambda i, be, nu: (layer, be[i], 0, 0)),
        ],
        out_specs=pl.BlockSpec((blk, d), lambda i, be, nu: (i, 0)),
        scratch_shapes=[pltpu.VMEM((d, de), BF16), pltpu.VMEM((d, de), BF16),
                        pltpu.VMEM((de, d), BF16)],
    )
    return pl.pallas_call(
        _expert_kernel,
        grid_spec=grid_spec,
        out_shape=jax.ShapeDtypeStruct((n_rows, d), F32),
        compiler_params=pltpu.CompilerParams(
            dimension_semantics=("arbitrary",), vmem_limit_bytes=VMEM_LIMIT),
        name="moe_experts",
    )(block_expert, n_used, xs, w_gate, w_up, w_down)


def _combine_kernel(d0_ref, d1_ref, y_hbm, h_ref, gate_ref, fw_ref, o_ref, ybuf, sem, *, final_norm):
    i = pl.program_id(0)
    nb = pl.num_programs(0)
    tc = h_ref.shape[0]

    def issue(b, slot):
        def body(rr, carry):
            for j in range(DMA_UNROLL):
                t = b * tc + rr * DMA_UNROLL + j
                for kk, d_ref in enumerate((d0_ref, d1_ref)):
                    pltpu.make_async_copy(y_hbm.at[pl.ds(d_ref[t], 1)],
                                          ybuf.at[slot, kk, rr, pl.ds(j, 1)],
                                          sem.at[slot]).start(priority=kk)
            return carry
        lax.fori_loop(0, tc // DMA_UNROLL, body, 0)

    @pl.when(i == 0)
    def _():
        issue(0, 0)

    @pl.when(i + 1 < nb)
    def _():
        issue(i + 1, (i + 1) % 2)

    slot = i % 2
    for kk in range(2):
        pltpu.make_async_copy(y_hbm.at[pl.ds(0, tc)], y_hbm.at[pl.ds(0, tc)], sem.at[slot]).wait()

    gate = gate_ref[...]
    y0 = ybuf[slot, 0].reshape(tc, h_ref.shape[1])
    y1 = ybuf[slot, 1].reshape(tc, h_ref.shape[1])
    o = h_ref[...] + (gate[:, 0:1] * y0 + gate[:, 1:2] * y1)
    if final_norm:
        o = _rms(o, fw_ref[...])
    o_ref[...] = o


def _combine(hf, y, dest0, dest1, gates, fw, final_norm):
    t, d = hf.shape
    tc = COMBINE_ROWS
    grid_spec = pltpu.PrefetchScalarGridSpec(
        num_scalar_prefetch=2,
        grid=(t // tc,),
        in_specs=[
            pl.BlockSpec(memory_space=pl.ANY),
            pl.BlockSpec((tc, d), lambda i, a, b: (i, 0)),
            pl.BlockSpec((tc, LANES), lambda i, a, b: (i, 0)),
            pl.BlockSpec((1, d), lambda i, a, b: (0, 0)),
        ],
        out_specs=pl.BlockSpec((tc, d), lambda i, a, b: (i, 0)),
        scratch_shapes=[pltpu.VMEM((2, 2, tc // SUBLANES, SUBLANES, d), F32),
                        pltpu.SemaphoreType.DMA((2,))],
    )
    return pl.pallas_call(
        functools.partial(_combine_kernel, final_norm=final_norm),
        grid_spec=grid_spec,
        out_shape=jax.ShapeDtypeStruct((t, d), F32),
        compiler_params=pltpu.CompilerParams(
            dimension_semantics=("arbitrary",), vmem_limit_bytes=VMEM_LIMIT),
        name="moe_combine",
    )(dest0, dest1, y, hf, gates, fw.reshape(1, d))


def _moe_layer(h, nw, grp_w, grp_b, exp_w, exp_b, w_gate, w_up, w_down, fw, layer, final_norm):
    b, s, d = h.shape
    t = b * s
    hf = h.reshape(t, d)
    rec, gates, cnt = _router(hf, nw, grp_w, grp_b, exp_w, exp_b)

    rec = rec.astype(jnp.int32)
    counts = cnt[0, :N_EXPERTS].astype(jnp.int32)
    padded = (counts + MOE_BLOCK - 1) // MOE_BLOCK * MOE_BLOCK
    pad_end = jnp.cumsum(padded)
    pad_start = pad_end - padded
    eids = jnp.arange(N_EXPERTS, dtype=jnp.int32)[:, None]
    dest0 = jnp.sum(jnp.where(rec[0][None, :] == eids, pad_start[:, None], 0), axis=0) + rec[2]
    dest1 = jnp.sum(jnp.where(rec[1][None, :] == eids, pad_start[:, None], 0), axis=0) + rec[3]
    n_rows = 2 * t + N_EXPERTS * MOE_BLOCK
    n_blocks = n_rows // MOE_BLOCK
    block_start = jnp.arange(n_blocks, dtype=jnp.int32) * MOE_BLOCK
    block_expert = jnp.minimum(
        jnp.sum((block_start[:, None] >= pad_end[None, :]).astype(jnp.int32), axis=1), N_EXPERTS - 1)
    n_used = (pad_end[-1:] // MOE_BLOCK).astype(jnp.int32)
    zfill = jnp.concatenate([jnp.where(padded > 0, pad_end - MOE_BLOCK, -1).astype(jnp.int32), n_used])

    xs = _dispatch(hf, nw, dest0, dest1, zfill, n_rows)
    y = _experts(xs, block_expert, n_used, w_gate, w_up, w_down, layer)
    out = _combine(hf, y, dest0, dest1, gates, fw, final_norm)
    return out.reshape(b, s, d)


def kernel(x, conv_norm_w, conv_pw1_w, conv_pw1_b, conv_dw_w, conv_dw_b, conv_ln_g, conv_ln_b,
           conv_pw2_w, conv_pw2_b, hgrn_norm_w, hgrn_w_in, hgrn_gnorm_w, hgrn_w_out, lower_bounds,
           ffn_norm_w, router_grp_w, router_grp_b, router_exp_w, router_exp_b,
           moe_w_gate, moe_w_up, moe_w_down, final_norm_w):
    depth = lower_bounds.shape[0]
    lb_p = jax.nn.softmax(lower_bounds.astype(F32), axis=0)
    lb_all = jnp.cumsum(lb_p, axis=0) - lb_p[0]
    h = x
    for layer in range(depth):
        j = layer // 2
        if layer % 2 == 0:
            h = _conformer_layer(h, conv_norm_w[j], conv_pw1_w[j], conv_pw1_b[j], conv_dw_w[j],
                                 conv_dw_b[j], conv_ln_g[j], conv_ln_b[j], conv_pw2_w[j],
                                 conv_pw2_b[j])
        else:
            h = _hgrn_layer(h, hgrn_norm_w[j], hgrn_w_in[j], hgrn_gnorm_w[j], hgrn_w_out[j],
                            lb_all[layer])
        h = _moe_layer(h, ffn_norm_w[layer], router_grp_w[layer], router_grp_b[layer],
                       router_exp_w[layer], router_exp_b[layer], moe_w_gate, moe_w_up,
                       moe_w_down, final_norm_w, layer=layer, final_norm=(layer == depth - 1))
    return h
```

```python
import functools

import jax
import jax.numpy as jnp
from jax import lax
from jax.experimental import pallas as pl
from jax.experimental.pallas import tpu as pltpu

F32 = jnp.float32
BF16 = jnp.bfloat16

NORM_EPS = 1e-6
CONV_WIDTH = 31
CONV_HALO = 32
HGRN_HEADS = 8
HEAD_DIM = 128
CHUNK = 64
N_GROUPS = 4
EXPERTS_PER_GROUP = 8
N_EXPERTS = N_GROUPS * EXPERTS_PER_GROUP
LANES = 128
SUBLANES = 8
GROUP_LANE0 = N_EXPERTS

CONV_ROWS = 256
CONV_RC = 128
HGRN_ROWS = 1024
ROUTER_ROWS = 512
MOE_BLOCK = 512
DMA_UNROLL = 8
DISPATCH_ROWS = 512
COMBINE_ROWS = 512
VMEM_LIMIT = 48 * 1024 * 1024


def _rms(x, w):
    ms = jnp.mean(x * x, axis=-1, keepdims=True)
    return x * lax.rsqrt(ms + NORM_EPS) * w


def _sigmoid(x):
    return 1.0 / (1.0 + jnp.exp(-x))


def _conv_kernel(x_ref, nw_ref, pw1_ref, b1_ref, dw_ref, dwb_ref, lng_ref, lnb_ref,
                 pw2_ref, b2_ref, o_ref, ubuf, cbuf):
    s = pl.program_id(1)
    ts = x_ref.shape[1]
    d = x_ref.shape[2]
    x = x_ref[0]
    hn = _rms(x, nw_ref[...]).astype(BF16)
    a = jnp.dot(hn, pw1_ref[...], preferred_element_type=F32) + b1_ref[...]
    u = a[:, :d] * _sigmoid(a[:, d:])

    @pl.when(s == 0)
    def _():
        ubuf[0:CONV_HALO, :] = jnp.zeros((CONV_HALO, d), F32)

    @pl.when(s > 0)
    def _():
        ubuf[0:CONV_HALO, :] = ubuf[ts:ts + CONV_HALO, :]

    ubuf[CONV_HALO:, :] = u

    off0 = CONV_HALO - (CONV_WIDTH - 1)
    win = CONV_RC + CONV_HALO
    for rc in range(ts // CONV_RC):
        for lc in range(d // LANES):
            ls = slice(lc * LANES, (lc + 1) * LANES)
            xw = ubuf[rc * CONV_RC:rc * CONV_RC + win, ls]
            acc = jnp.broadcast_to(dwb_ref[:, ls], (CONV_RC, LANES))
            for p in range(SUBLANES):
                xp = xw if p == 0 else pltpu.roll(xw, win - p, axis=0)
                for j in range(CONV_WIDTH):
                    if (off0 + j) % SUBLANES == p:
                        a0 = off0 + j - p
                        acc = acc + dw_ref[j:j + 1, ls] * xp[a0:a0 + CONV_RC, :]
            cbuf[rc * CONV_RC:(rc + 1) * CONV_RC, ls] = acc

    c = cbuf[...]
    mu = jnp.mean(c, axis=-1, keepdims=True)
    cc = c - mu
    var = jnp.mean(cc * cc, axis=-1, keepdims=True)
    n = cc * lax.rsqrt(var + NORM_EPS) * lng_ref[...] + lnb_ref[...]
    sw = (n * _sigmoid(n)).astype(BF16)
    y = jnp.dot(sw, pw2_ref[...], preferred_element_type=F32) + b2_ref[...]
    o_ref[0] = x + y


def _conformer_layer(h, nw, pw1, b1, dw, dwb, lng, lnb, pw2, b2):
    b, s, d = h.shape
    ts = CONV_ROWS
    row = lambda v: v.reshape(1, -1)
    full = lambda shape: pl.BlockSpec(shape, lambda bi, si: (0,) * len(shape))
    return pl.pallas_call(
        _conv_kernel,
        grid=(b, s // ts),
        in_specs=[
            pl.BlockSpec((1, ts, d), lambda bi, si: (bi, si, 0)),
            full((1, d)), full((d, 2 * d)), full((1, 2 * d)),
            full((CONV_WIDTH, d)), full((1, d)), full((1, d)), full((1, d)),
            full((d, d)), full((1, d)),
        ],
        out_specs=pl.BlockSpec((1, ts, d), lambda bi, si: (bi, si, 0)),
        out_shape=jax.ShapeDtypeStruct((b, s, d), F32),
        scratch_shapes=[pltpu.VMEM((ts + CONV_HALO, d), F32), pltpu.VMEM((ts, d), F32)],
        compiler_params=pltpu.CompilerParams(
            dimension_semantics=("arbitrary", "arbitrary"), vmem_limit_bytes=VMEM_LIMIT),
        name="conformer_conv",
    )(h, row(nw), pw1.astype(BF16), row(b1), dw, row(dwb), row(lng), row(lnb),
      pw2.astype(BF16), row(b2))


HGRN_GROUP = 4
GROUP_ROWS = HGRN_GROUP * CHUNK


def _hgrn_state_updates(gi, k_s, v_s, g_s, upd_s):
    rows = pl.ds(pl.multiple_of(gi * GROUP_ROWS, GROUP_ROWS), GROUP_ROWS)
    gc = g_s[rows, :]
    k = k_s[rows, :]
    chunk_of_row = lax.broadcasted_iota(jnp.int32, (GROUP_ROWS, HEAD_DIM), 0) // CHUNK
    glast = jnp.concatenate(
        [jnp.broadcast_to(gc[(j + 1) * CHUNK - 1:(j + 1) * CHUNK, :], (CHUNK, HEAD_DIM))
         for j in range(HGRN_GROUP)], axis=0)
    kd = k * jnp.exp(glast - gc)
    rhs = jnp.concatenate([jnp.where(chunk_of_row == j, kd, 0.0).astype(BF16)
                           for j in range(HGRN_GROUP)], axis=1)
    vb = v_s[rows, :].astype(BF16)
    upd = lax.dot_general(vb, rhs, (((0,), (0,)), ((), ())), preferred_element_type=F32)
    for j in range(HGRN_GROUP):
        upd_s[gi * HGRN_GROUP + j] = upd[:, j * HEAD_DIM:(j + 1) * HEAD_DIM]


def _hgrn_group(gi, q_s, k_s, v_s, g_s, o_s, sb_s):
    n = GROUP_ROWS
    rows = pl.ds(pl.multiple_of(gi * n, n), n)
    q = q_s[rows, :]
    k = k_s[rows, :]
    gc = g_s[rows, :]
    vb = v_s[rows, :].astype(BF16)
    nt = (((1,), (1,)), ((), ()))

    qg = (q * jnp.exp(gc)).astype(BF16)
    sts = sb_s[pl.ds(pl.multiple_of(gi * HGRN_GROUP * HEAD_DIM, HGRN_GROUP * HEAD_DIM),
                     HGRN_GROUP * HEAD_DIM), :]
    wide = lax.dot_general(qg, sts, nt, preferred_element_type=F32)
    inter = jnp.concatenate(
        [wide[j * CHUNK:(j + 1) * CHUNK, j * HEAD_DIM:(j + 1) * HEAD_DIM]
         for j in range(HGRN_GROUP)], axis=0)

    row = lax.broadcasted_iota(jnp.int32, (n, HEAD_DIM), 0)
    ti = lax.broadcasted_iota(jnp.int32, (n, n), 0)
    si = lax.broadcasted_iota(jnp.int32, (n, n), 1)
    tx = ti ^ si
    scores = jnp.broadcast_to(jnp.sum(q * k, axis=-1, keepdims=True), (n, n))
    end = gc
    b = 1
    while b < CHUNK:
        upper = (row & b) != 0
        e = jnp.exp(jnp.where(upper, gc - pltpu.roll(end, b, axis=0), end - gc))
        m = (jnp.where(upper, q, k) * e).astype(BF16)
        sb = lax.dot_general(m, m, nt, preferred_element_type=F32)
        scores = jnp.where(tx >= b, sb, scores)
        end = jnp.where(upper, end, pltpu.roll(end, n - b, axis=0))
        b *= 2
    scores = jnp.where((ti >= si) & (tx < CHUNK), scores, 0.0)

    intra = jnp.dot(scores.astype(BF16), vb, preferred_element_type=F32)
    o_s[rows, :] = inter + intra


def _hgrn_kernel(x_ref, nw_ref, win_ref, lb_ref, gn_ref, wout_ref, o_ref,
                 hn_s, proj_s, q_s, k_s, v_s, g_s, z_s, o_s, oh_s, upd_s, sb_s, st_ref):
    s = pl.program_id(1)
    h = pl.program_id(2)
    nh = pl.num_programs(2)
    ts = x_ref.shape[1]

    @pl.when(h == 0)
    def _():
        hn_s[...] = _rms(x_ref[0], nw_ref[...]).astype(BF16)
        proj_s[...] = jnp.dot(hn_s[...], win_ref[0], preferred_element_type=F32)

    @pl.when(s == 0)
    def _():
        st_ref[h] = jnp.zeros((HEAD_DIM, HEAD_DIM), F32)

    proj = proj_s[...]
    qr = proj[:, 0:HEAD_DIM]
    q_s[...] = qr * _sigmoid(qr)
    lb = lb_ref[0]
    forget = lb + (1.0 - lb) * _sigmoid(proj[:, HEAD_DIM:2 * HEAD_DIM])
    k_s[...] = 1.0 - forget
    v_s[...] = proj[:, 2 * HEAD_DIM:3 * HEAD_DIM]
    zr = proj[:, 3 * HEAD_DIM:4 * HEAD_DIM]
    z_s[...] = zr * _sigmoid(zr)

    g = jnp.log(forget)
    rmod = lax.broadcasted_iota(jnp.int32, (ts, 1), 0) % CHUNK
    dstep = 1
    while dstep < CHUNK:
        g = g + jnp.where(rmod >= dstep, pltpu.roll(g, dstep, axis=0), 0.0)
        dstep *= 2
    g_s[...] = g

    proj_s[...] = jnp.dot(hn_s[...], win_ref[jnp.minimum(h + 1, nh - 1)],
                          preferred_element_type=F32)

    nchunk = ts // CHUNK
    ngroup = ts // GROUP_ROWS
    for gi in range(ngroup):
        _hgrn_state_updates(gi, k_s, v_s, g_s, upd_s)

    st = st_ref[h]
    for c in range(nchunk):
        sb_s[c * HEAD_DIM:(c + 1) * HEAD_DIM, :] = st.astype(BF16)
        glast = g_s[(c + 1) * CHUNK - 1:(c + 1) * CHUNK, :]
        st = jnp.exp(glast) * st + upd_s[c]
    st_ref[h] = st

    for gi in range(ngroup):
        _hgrn_group(gi, q_s, k_s, v_s, g_s, o_s, sb_s)

    o = o_s[...]
    o = o * lax.rsqrt(jnp.mean(o * o, axis=-1, keepdims=True) + NORM_EPS) * gn_ref[...]
    oh_s[h] = (o * z_s[...]).astype(BF16)

    @pl.when(h == pl.num_programs(2) - 1)
    def _():
        oall = jnp.concatenate([oh_s[hh] for hh in range(HGRN_HEADS)], axis=1)
        o_ref[0] = x_ref[0] + jnp.dot(oall, wout_ref[...], preferred_element_type=F32)


def _hgrn_layer(h, nw, w_in, gn, w_out, lb):
    b, s, d = h.shape
    ts = HGRN_ROWS
    nh = HGRN_HEADS
    win_r = w_in.reshape(d, 4, nh, HEAD_DIM).transpose(2, 0, 1, 3).reshape(nh, d, 4 * HEAD_DIM)
    return pl.pallas_call(
        _hgrn_kernel,
        grid=(b, s // ts, nh),
        in_specs=[
            pl.BlockSpec((1, ts, d), lambda bi, si, hi: (bi, si, 0)),
            pl.BlockSpec((1, d), lambda bi, si, hi: (0, 0)),
            pl.BlockSpec((nh, d, 4 * HEAD_DIM), lambda bi, si, hi: (0, 0, 0),
                         pipeline_mode=pl.Buffered(1)),
            pl.BlockSpec((1, 1, HEAD_DIM), lambda bi, si, hi: (hi, 0, 0)),
            pl.BlockSpec((1, HEAD_DIM), lambda bi, si, hi: (0, 0)),
            pl.BlockSpec((d, d), lambda bi, si, hi: (0, 0), pipeline_mode=pl.Buffered(1)),
        ],
        out_specs=pl.BlockSpec((1, ts, d), lambda bi, si, hi: (bi, si, 0)),
        out_shape=jax.ShapeDtypeStruct((b, s, d), F32),
        scratch_shapes=[
            pltpu.VMEM((ts, d), BF16),
            pltpu.VMEM((ts, 4 * HEAD_DIM), F32),
            pltpu.VMEM((ts, HEAD_DIM), F32), pltpu.VMEM((ts, HEAD_DIM), F32),
            pltpu.VMEM((ts, HEAD_DIM), F32), pltpu.VMEM((ts, HEAD_DIM), F32),
            pltpu.VMEM((ts, HEAD_DIM), F32), pltpu.VMEM((ts, HEAD_DIM), F32),
            pltpu.VMEM((nh, ts, HEAD_DIM), BF16),
            pltpu.VMEM((ts // CHUNK, HEAD_DIM, HEAD_DIM), F32),
            pltpu.VMEM((ts // CHUNK * HEAD_DIM, HEAD_DIM), BF16),
            pltpu.VMEM((nh, HEAD_DIM, HEAD_DIM), F32),
        ],
        compiler_params=pltpu.CompilerParams(
            dimension_semantics=("arbitrary", "arbitrary", "arbitrary"),
            vmem_limit_bytes=VMEM_LIMIT),
        name="hgrn2",
    )(h, nw.reshape(1, d), win_r.astype(BF16), lb.reshape(nh, 1, HEAD_DIM),
      gn.reshape(1, HEAD_DIM), w_out.astype(BF16))


def _router_kernel(h_ref, nw_ref, wr_ref, br_ref, idx_ref, gate_ref, cnt_ref, cnt_s):
    i = pl.program_id(0)
    tm = h_ref.shape[0]

    @pl.when(i == 0)
    def _():
        cnt_s[...] = jnp.zeros_like(cnt_s)

    hn = _rms(h_ref[...], nw_ref[...])
    h1 = hn.astype(BF16)
    h2 = (hn - h1.astype(F32)).astype(BF16)
    w = wr_ref[...]
    w1 = w.astype(BF16)
    w2 = (w - w1.astype(F32)).astype(BF16)
    logits = (jnp.dot(h1, w1, preferred_element_type=F32) + jnp.dot(h1, w2, preferred_element_type=F32)
              + jnp.dot(h2, w1, preferred_element_type=F32)) + br_ref[...]
    li = lax.broadcasted_iota(jnp.int32, (tm, LANES), 1).astype(F32)
    neg = jnp.float32(-jnp.inf)
    big = jnp.float32(LANES)

    gl = jnp.where((li >= GROUP_LANE0) & (li < GROUP_LANE0 + N_GROUPS), logits, neg)
    gmax = jnp.max(gl, axis=-1, keepdims=True)
    gval = 1.0 / jnp.sum(jnp.exp(gl - gmax), axis=-1, keepdims=True)
    gidx = jnp.min(jnp.where(gl == gmax, li, big), axis=-1, keepdims=True) - GROUP_LANE0

    lo = gidx * EXPERTS_PER_GROUP
    el = jnp.where((li >= lo) & (li < lo + EXPERTS_PER_GROUP), logits, neg)
    v0 = jnp.max(el, axis=-1, keepdims=True)
    i0 = jnp.min(jnp.where(el == v0, li, big), axis=-1, keepdims=True)
    el2 = jnp.where(li == i0, neg, el)
    v1 = jnp.max(el2, axis=-1, keepdims=True)
    i1 = jnp.min(jnp.where(el2 == v1, li, big), axis=-1, keepdims=True)
    t = jnp.exp(v1 - v0)
    p0 = 1.0 / (1.0 + t)
    g0 = gval * p0
    g1 = gval * (t * p0)

    oh = jnp.where((li == i0) | (li == i1), 1.0, 0.0)
    rr = lax.broadcasted_iota(jnp.int32, (tm, tm), 0)
    cc = lax.broadcasted_iota(jnp.int32, (tm, tm), 1)
    lower = jnp.where(rr > cc, 1.0, 0.0).astype(BF16)
    before = jnp.dot(lower, oh.astype(BF16), preferred_element_type=F32) + cnt_s[...]
    r0 = jnp.sum(jnp.where(li == i0, before, 0.0), axis=-1, keepdims=True)
    r1 = jnp.sum(jnp.where(li == i1, before, 0.0), axis=-1, keepdims=True)
    cnt_s[...] = cnt_s[...] + jnp.sum(oh, axis=0, keepdims=True)

    rec = jnp.where(li == 0.0, i0, jnp.where(li == 1.0, i1,
                    jnp.where(li == 2.0, r0, jnp.where(li == 3.0, r1, 0.0))))
    idx_ref[...] = rec.T[0:8, :]
    gate_ref[...] = jnp.where(li == 0.0, g0, jnp.where(li == 1.0, g1, 0.0))
    cnt_ref[...] = cnt_s[...]


def _router(hf, nw, grp_w, grp_b, exp_w, exp_b):
    t, d = hf.shape
    tm = ROUTER_ROWS
    wr = jnp.zeros((d, LANES), F32)
    wr = wr.at[:, :N_EXPERTS].set(exp_w.reshape(d, N_EXPERTS))
    wr = wr.at[:, GROUP_LANE0:GROUP_LANE0 + N_GROUPS].set(grp_w)
    br = jnp.zeros((1, LANES), F32)
    br = br.at[0, :N_EXPERTS].set(exp_b.reshape(N_EXPERTS))
    br = br.at[0, GROUP_LANE0:GROUP_LANE0 + N_GROUPS].set(grp_b)
    return pl.pallas_call(
        _router_kernel,
        grid=(t // tm,),
        in_specs=[
            pl.BlockSpec((tm, d), lambda i: (i, 0)),
            pl.BlockSpec((1, d), lambda i: (0, 0)),
            pl.BlockSpec((d, LANES), lambda i: (0, 0)),
            pl.BlockSpec((1, LANES), lambda i: (0, 0)),
        ],
        out_specs=[
            pl.BlockSpec((8, tm), lambda i: (0, i)),
            pl.BlockSpec((tm, LANES), lambda i: (i, 0)),
            pl.BlockSpec((1, LANES), lambda i: (0, 0)),
        ],
        out_shape=[
            jax.ShapeDtypeStruct((8, t), F32),
            jax.ShapeDtypeStruct((t, LANES), F32),
            jax.ShapeDtypeStruct((1, LANES), F32),
        ],
        scratch_shapes=[pltpu.VMEM((1, LANES), F32)],
        compiler_params=pltpu.CompilerParams(
            dimension_semantics=("arbitrary",), vmem_limit_bytes=VMEM_LIMIT),
        name="moe_router",
    )(hf, nw.reshape(1, d), wr, br)


def _store_row_tiles(ref, x):
    n = x.shape[0]
    for j in range(SUBLANES):
        ref[pl.ds(j, n, stride=SUBLANES), :] = x[:, j * LANES:(j + 1) * LANES]


def _load_row_tiles(ref):
    n = ref.shape[0] // SUBLANES
    return jnp.concatenate([ref[pl.ds(j, n, stride=SUBLANES), :] for j in range(SUBLANES)], axis=1)


def _row_tile(ref, r):
    return ref.at[pl.ds(pl.multiple_of(r * SUBLANES, SUBLANES), SUBLANES)]


def _dispatch_kernel(d0_ref, d1_ref, zf_ref, h_ref, nw_ref, xs_hbm, stage, zbuf, sem, zsem):
    i = pl.program_id(0)
    nb = pl.num_programs(0)
    tm = h_ref.shape[0]
    blk = zbuf.shape[0]
    slot = i % 2

    def wait_slot(s):
        for _ in range(2):
            pltpu.make_async_copy(xs_hbm.at[pl.ds(0, tm)], xs_hbm.at[pl.ds(0, tm)], sem.at[s]).wait()

    @pl.when(i == 0)
    def _():
        zbuf[...] = jnp.zeros_like(zbuf)

        def zbody(e, carry):
            @pl.when(zf_ref[e] >= 0)
            def _():
                start = pl.multiple_of(zf_ref[e], blk)
                cp = pltpu.make_async_copy(zbuf, xs_hbm.at[pl.ds(start, blk)], zsem)
                cp.start()
                cp.wait()
            return carry
        lax.fori_loop(0, N_EXPERTS, zbody, 0)

        def tbody(b, carry):
            cp = pltpu.make_async_copy(zbuf, xs_hbm.at[pl.ds(pl.multiple_of(b * blk, blk), blk)], zsem)
            cp.start()
            cp.wait()
            return carry
        lax.fori_loop(zf_ref[N_EXPERTS], xs_hbm.shape[0] // blk, tbody, 0)

    @pl.when(i >= 2)
    def _():
        wait_slot(slot)

    _store_row_tiles(stage.at[slot], _rms(h_ref[...], nw_ref[...]))

    def body(rr, carry):
        for j in range(DMA_UNROLL):
            r = rr * DMA_UNROLL + j
            for d_ref in (d0_ref, d1_ref):
                pltpu.make_async_copy(_row_tile(stage.at[slot], r), xs_hbm.at[d_ref[i * tm + r]],
                                      sem.at[slot]).start()
        return carry
    lax.fori_loop(0, tm // DMA_UNROLL, body, 0)

    @pl.when(i == nb - 1)
    def _():
        wait_slot(slot)

        @pl.when(nb >= 2)
        def _():
            wait_slot(1 - slot)


def _dispatch(hf, nw, dest0, dest1, zfill, n_rows):
    t, d = hf.shape
    tm = DISPATCH_ROWS
    grid_spec = pltpu.PrefetchScalarGridSpec(
        num_scalar_prefetch=3,
        grid=(t // tm,),
        in_specs=[
            pl.BlockSpec((tm, d), lambda i, a, b, c: (i, 0)),
            pl.BlockSpec((1, d), lambda i, a, b, c: (0, 0)),
        ],
        out_specs=pl.BlockSpec(memory_space=pl.ANY),
        scratch_shapes=[pltpu.VMEM((2, tm * SUBLANES, d // SUBLANES), F32),
                        pltpu.VMEM((MOE_BLOCK, SUBLANES, d // SUBLANES), F32),
                        pltpu.SemaphoreType.DMA((2,)), pltpu.SemaphoreType.DMA(())],
    )
    return pl.pallas_call(
        _dispatch_kernel,
        grid_spec=grid_spec,
        out_shape=jax.ShapeDtypeStruct((n_rows, SUBLANES, d // SUBLANES), F32),
        compiler_params=pltpu.CompilerParams(
            dimension_semantics=("arbitrary",), vmem_limit_bytes=VMEM_LIMIT),
        name="moe_dispatch",
    )(dest0, dest1, zfill, hf, nw.reshape(1, d))


def _expert_kernel(be_ref, nu_ref, x_ref, wg_ref, wu_ref, wd_ref, y_ref, wg_s, wu_s, wd_s):
    i = pl.program_id(0)
    prev = be_ref[jnp.maximum(i - 1, 0)]

    @pl.when(i < nu_ref[0])
    def _():
        @pl.when((i == 0) | (be_ref[i] != prev))
        def _():
            wg_s[...] = wg_ref[0, 0].astype(BF16)
            wu_s[...] = wu_ref[0, 0].astype(BF16)
            wd_s[...] = wd_ref[0, 0].astype(BF16)

        hn = _load_row_tiles(x_ref).astype(BF16)
        g = jnp.dot(hn, wg_s[...], preferred_element_type=F32)
        u = jnp.dot(hn, wu_s[...], preferred_element_type=F32)
        a = (g * _sigmoid(g) * u).astype(BF16)
        _store_row_tiles(y_ref, jnp.dot(a, wd_s[...], preferred_element_type=F32))

    @pl.when(i >= nu_ref[0])
    def _():
        y_ref[...] = jnp.zeros_like(y_ref)


def _experts(xs, block_expert, n_used, w_gate, w_up, w_down, layer):
    n_rows = xs.shape[0]
    blk = MOE_BLOCK
    d, de = w_gate.shape[-2:]
    row_tile = (blk * SUBLANES, d // SUBLANES)
    used_blk = lambda i, be, nu: (jnp.minimum(i, nu[0] - 1), 0)
    grid_spec = pltpu.PrefetchScalarGridSpec(
        num_scalar_prefetch=2,
        grid=(n_rows // blk,),
        in_specs=[
            pl.BlockSpec(row_tile, used_blk),
            pl.BlockSpec((1, 1, d, de), lambda i, be, nu: (layer, be[i], 0, 0)),
            pl.BlockSpec((1, 1, d, de), lambda i, be, nu: (layer, be[i], 0, 0)),
            pl.BlockSpec((1, 1, de, d), lambda i, be, nu: (layer, be[i], 0, 0)),
        ],
        out_specs=pl.BlockSpec(row_tile, lambda i, be, nu: (i, 0)),
        scratch_shapes=[pltpu.VMEM((d, de), BF16), pltpu.VMEM((d, de), BF16),
                        pltpu.VMEM((de, d), BF16)],
    )
    return pl.pallas_call(
        _expert_kernel,
        grid_spec=grid_spec,
        out_shape=jax.ShapeDtypeStruct((n_rows * SUBLANES, d // SUBLANES), F32),
        compiler_params=pltpu.CompilerParams(
            dimension_semantics=("arbitrary",), vmem_limit_bytes=VMEM_LIMIT),
        name="moe_experts",
    )(block_expert, n_used, xs.reshape(n_rows * SUBLANES, d // SUBLANES), w_gate, w_up, w_down)


def _combine_kernel(d0_ref, d1_ref, y_hbm, h_ref, gate_ref, fw_ref, o_ref, ybuf, sem, *, final_norm):
    i = pl.program_id(0)
    nb = pl.num_programs(0)
    tc = h_ref.shape[0]

    def issue(b, slot):
        def body(rr, carry):
            for j in range(DMA_UNROLL):
                r = rr * DMA_UNROLL + j
                for kk, d_ref in enumerate((d0_ref, d1_ref)):
                    pltpu.make_async_copy(y_hbm.at[d_ref[b * tc + r]], _row_tile(ybuf.at[slot, kk], r),
                                          sem.at[slot]).start()
            return carry
        lax.fori_loop(0, tc // DMA_UNROLL, body, 0)

    @pl.when(i == 0)
    def _():
        issue(0, 0)

    @pl.when(i + 1 < nb)
    def _():
        issue(i + 1, (i + 1) % 2)

    slot = i % 2
    for kk in range(2):
        pltpu.make_async_copy(y_hbm.at[pl.ds(0, tc)], y_hbm.at[pl.ds(0, tc)], sem.at[slot]).wait()

    gate = gate_ref[...]
    y0 = _load_row_tiles(ybuf.at[slot, 0])
    y1 = _load_row_tiles(ybuf.at[slot, 1])
    o = h_ref[...] + (gate[:, 0:1] * y0 + gate[:, 1:2] * y1)
    if final_norm:
        o = _rms(o, fw_ref[...])
    o_ref[...] = o


def _combine(hf, y, dest0, dest1, gates, fw, final_norm):
    t, d = hf.shape
    tc = COMBINE_ROWS
    grid_spec = pltpu.PrefetchScalarGridSpec(
        num_scalar_prefetch=2,
        grid=(t // tc,),
        in_specs=[
            pl.BlockSpec(memory_space=pl.ANY),
            pl.BlockSpec((tc, d), lambda i, a, b: (i, 0)),
            pl.BlockSpec((tc, LANES), lambda i, a, b: (i, 0)),
            pl.BlockSpec((1, d), lambda i, a, b: (0, 0)),
        ],
        out_specs=pl.BlockSpec((tc, d), lambda i, a, b: (i, 0)),
        scratch_shapes=[pltpu.VMEM((2, 2, tc * SUBLANES, d // SUBLANES), F32),
                        pltpu.SemaphoreType.DMA((2,))],
    )
    return pl.pallas_call(
        functools.partial(_combine_kernel, final_norm=final_norm),
        grid_spec=grid_spec,
        out_shape=jax.ShapeDtypeStruct((t, d), F32),
        compiler_params=pltpu.CompilerParams(
            dimension_semantics=("arbitrary",), vmem_limit_bytes=VMEM_LIMIT),
        name="moe_combine",
    )(dest0, dest1, y.reshape(-1, SUBLANES, d // SUBLANES), hf, gates, fw.reshape(1, d))


def _moe_layer(h, nw, grp_w, grp_b, exp_w, exp_b, w_gate, w_up, w_down, fw, layer, final_norm):
    b, s, d = h.shape
    t = b * s
    hf = h.reshape(t, d)
    rec, gates, cnt = _router(hf, nw, grp_w, grp_b, exp_w, exp_b)

    rec = rec.astype(jnp.int32)
    counts = cnt[0, :N_EXPERTS].astype(jnp.int32)
    padded = (counts + MOE_BLOCK - 1) // MOE_BLOCK * MOE_BLOCK
    pad_end = jnp.cumsum(padded)
    pad_start = pad_end - padded
    eids = jnp.arange(N_EXPERTS, dtype=jnp.int32)[:, None]
    dest0 = jnp.sum(jnp.where(rec[0][None, :] == eids, pad_start[:, None], 0), axis=0) + rec[2]
    dest1 = jnp.sum(jnp.where(rec[1][None, :] == eids, pad_start[:, None], 0), axis=0) + rec[3]
    n_rows = 2 * t + N_EXPERTS * MOE_BLOCK
    n_blocks = n_rows // MOE_BLOCK
    block_start = jnp.arange(n_blocks, dtype=jnp.int32) * MOE_BLOCK
    block_expert = jnp.minimum(
        jnp.sum((block_start[:, None] >= pad_end[None, :]).astype(jnp.int32), axis=1), N_EXPERTS - 1)
    n_used = (pad_end[-1:] // MOE_BLOCK).astype(jnp.int32)
    zfill = jnp.concatenate([jnp.where(padded > 0, pad_end - MOE_BLOCK, -1).astype(jnp.int32), n_used])

    xs = _dispatch(hf, nw, dest0, dest1, zfill, n_rows)
    y = _experts(xs, block_expert, n_used, w_gate, w_up, w_down, layer)
    out = _combine(hf, y, dest0, dest1, gates, fw, final_norm)
    return out.reshape(b, s, d)


def kernel(x, conv_norm_w, conv_pw1_w, conv_pw1_b, conv_dw_w, conv_dw_b, conv_ln_g, conv_ln_b,
           conv_pw2_w, conv_pw2_b, hgrn_norm_w, hgrn_w_in, hgrn_gnorm_w, hgrn_w_out, lower_bounds,
           ffn_norm_w, router_grp_w, router_grp_b, router_exp_w, router_exp_b,
           moe_w_gate, moe_w_up, moe_w_down, final_norm_w):
    depth = lower_bounds.shape[0]
    lb_p = jax.nn.softmax(lower_bounds.astype(F32), axis=0)
    lb_all = jnp.cumsum(lb_p, axis=0) - lb_p[0]
    h = x
    for layer in range(depth):
        j = layer // 2
        if layer % 2 == 0:
            h = _conformer_layer(h, conv_norm_w[j], conv_pw1_w[j], conv_pw1_b[j], conv_dw_w[j],
                                 conv_dw_b[j], conv_ln_g[j], conv_ln_b[j], conv_pw2_w[j],
                                 conv_pw2_b[j])
        else:
            h = _hgrn_layer(h, hgrn_norm_w[j], hgrn_w_in[j], hgrn_gnorm_w[j], hgrn_w_out[j],
                            lb_all[layer])
        h = _moe_layer(h, ffn_norm_w[layer], router_grp_w[layer], router_grp_b[layer],
                       router_exp_w[layer], router_exp_b[layer], moe_w_gate, moe_w_up,
                       moe_w_down, final_norm_w, layer=layer, final_norm=(layer == depth - 1))
    return h
```

```python
import functools

import jax
import jax.numpy as jnp
from jax import lax
from jax.experimental import pallas as pl
from jax.experimental.pallas import tpu as pltpu

F32 = jnp.float32
BF16 = jnp.bfloat16

NORM_EPS = 1e-6
CONV_WIDTH = 31
CONV_HALO = 32
HGRN_HEADS = 8
HEAD_DIM = 128
CHUNK = 64
N_GROUPS = 4
EXPERTS_PER_GROUP = 8
N_EXPERTS = N_GROUPS * EXPERTS_PER_GROUP
LANES = 128
SUBLANES = 8
GROUP_LANE0 = N_EXPERTS

CONV_ROWS = 256
CONV_RC = 128
HGRN_ROWS = 1024
ROUTER_ROWS = 512
MOE_BLOCK = 512
DMA_UNROLL = 8
DISPATCH_ROWS = 512
COMBINE_ROWS = 512
VMEM_LIMIT = 48 * 1024 * 1024


def _rms(x, w):
    ms = jnp.mean(x * x, axis=-1, keepdims=True)
    return x * lax.rsqrt(ms + NORM_EPS) * w


def _sigmoid(x):
    return 1.0 / (1.0 + jnp.exp(-x))


def _conv_kernel(x_ref, nw_ref, pw1_ref, b1_ref, dw_ref, dwb_ref, lng_ref, lnb_ref,
                 pw2_ref, b2_ref, o_ref, ubuf, cbuf):
    s = pl.program_id(1)
    ts = x_ref.shape[1]
    d = x_ref.shape[2]
    x = x_ref[0]
    hn = _rms(x, nw_ref[...]).astype(BF16)
    a = jnp.dot(hn, pw1_ref[...], preferred_element_type=F32) + b1_ref[...]
    u = a[:, :d] * _sigmoid(a[:, d:])

    @pl.when(s == 0)
    def _():
        ubuf[0:CONV_HALO, :] = jnp.zeros((CONV_HALO, d), F32)

    @pl.when(s > 0)
    def _():
        ubuf[0:CONV_HALO, :] = ubuf[ts:ts + CONV_HALO, :]

    ubuf[CONV_HALO:, :] = u

    off0 = CONV_HALO - (CONV_WIDTH - 1)
    win = CONV_RC + CONV_HALO
    for rc in range(ts // CONV_RC):
        for lc in range(d // LANES):
            ls = slice(lc * LANES, (lc + 1) * LANES)
            xw = ubuf[rc * CONV_RC:rc * CONV_RC + win, ls]
            acc = jnp.broadcast_to(dwb_ref[:, ls], (CONV_RC, LANES))
            for p in range(SUBLANES):
                xp = xw if p == 0 else pltpu.roll(xw, win - p, axis=0)
                for j in range(CONV_WIDTH):
                    if (off0 + j) % SUBLANES == p:
                        a0 = off0 + j - p
                        acc = acc + dw_ref[j:j + 1, ls] * xp[a0:a0 + CONV_RC, :]
            cbuf[rc * CONV_RC:(rc + 1) * CONV_RC, ls] = acc

    c = cbuf[...]
    mu = jnp.mean(c, axis=-1, keepdims=True)
    cc = c - mu
    var = jnp.mean(cc * cc, axis=-1, keepdims=True)
    n = cc * lax.rsqrt(var + NORM_EPS) * lng_ref[...] + lnb_ref[...]
    sw = (n * _sigmoid(n)).astype(BF16)
    y = jnp.dot(sw, pw2_ref[...], preferred_element_type=F32) + b2_ref[...]
    o_ref[0] = x + y


def _conformer_layer(h, nw, pw1, b1, dw, dwb, lng, lnb, pw2, b2):
    b, s, d = h.shape
    ts = CONV_ROWS
    row = lambda v: v.reshape(1, -1)
    full = lambda shape: pl.BlockSpec(shape, lambda bi, si: (0,) * len(shape))
    return pl.pallas_call(
        _conv_kernel,
        grid=(b, s // ts),
        in_specs=[
            pl.BlockSpec((1, ts, d), lambda bi, si: (bi, si, 0)),
            full((1, d)), full((d, 2 * d)), full((1, 2 * d)),
            full((CONV_WIDTH, d)), full((1, d)), full((1, d)), full((1, d)),
            full((d, d)), full((1, d)),
        ],
        out_specs=pl.BlockSpec((1, ts, d), lambda bi, si: (bi, si, 0)),
        out_shape=jax.ShapeDtypeStruct((b, s, d), F32),
        scratch_shapes=[pltpu.VMEM((ts + CONV_HALO, d), F32), pltpu.VMEM((ts, d), F32)],
        compiler_params=pltpu.CompilerParams(
            dimension_semantics=("arbitrary", "arbitrary"), vmem_limit_bytes=VMEM_LIMIT),
        name="conformer_conv",
    )(h, row(nw), pw1.astype(BF16), row(b1), dw, row(dwb), row(lng), row(lnb),
      pw2.astype(BF16), row(b2))


HGRN_GROUP = 4
GROUP_ROWS = HGRN_GROUP * CHUNK


def _hgrn_state_updates(gi, k_s, v_s, g_s, upd_s):
    rows = pl.ds(pl.multiple_of(gi * GROUP_ROWS, GROUP_ROWS), GROUP_ROWS)
    gc = g_s[rows, :]
    k = k_s[rows, :]
    chunk_of_row = lax.broadcasted_iota(jnp.int32, (GROUP_ROWS, HEAD_DIM), 0) // CHUNK
    glast = jnp.concatenate(
        [jnp.broadcast_to(gc[(j + 1) * CHUNK - 1:(j + 1) * CHUNK, :], (CHUNK, HEAD_DIM))
         for j in range(HGRN_GROUP)], axis=0)
    kd = k * jnp.exp(glast - gc)
    rhs = jnp.concatenate([jnp.where(chunk_of_row == j, kd, 0.0).astype(BF16)
                           for j in range(HGRN_GROUP)], axis=1)
    vb = v_s[rows, :].astype(BF16)
    upd = lax.dot_general(vb, rhs, (((0,), (0,)), ((), ())), preferred_element_type=F32)
    for j in range(HGRN_GROUP):
        upd_s[gi * HGRN_GROUP + j] = upd[:, j * HEAD_DIM:(j + 1) * HEAD_DIM]


def _hgrn_group(gi, q_s, k_s, v_s, g_s, o_s, sb_s):
    n = GROUP_ROWS
    rows = pl.ds(pl.multiple_of(gi * n, n), n)
    q = q_s[rows, :]
    k = k_s[rows, :]
    gc = g_s[rows, :]
    vb = v_s[rows, :].astype(BF16)
    nt = (((1,), (1,)), ((), ()))

    qg = (q * jnp.exp(gc)).astype(BF16)
    sts = sb_s[pl.ds(pl.multiple_of(gi * HGRN_GROUP * HEAD_DIM, HGRN_GROUP * HEAD_DIM),
                     HGRN_GROUP * HEAD_DIM), :]
    wide = lax.dot_general(qg, sts, nt, preferred_element_type=F32)
    inter = jnp.concatenate(
        [wide[j * CHUNK:(j + 1) * CHUNK, j * HEAD_DIM:(j + 1) * HEAD_DIM]
         for j in range(HGRN_GROUP)], axis=0)

    row = lax.broadcasted_iota(jnp.int32, (n, HEAD_DIM), 0)
    ti = lax.broadcasted_iota(jnp.int32, (n, n), 0)
    si = lax.broadcasted_iota(jnp.int32, (n, n), 1)
    tx = ti ^ si
    scores = jnp.broadcast_to(jnp.sum(q * k, axis=-1, keepdims=True), (n, n))
    end = gc
    b = 1
    while b < CHUNK:
        upper = (row & b) != 0
        e = jnp.exp(jnp.where(upper, gc - pltpu.roll(end, b, axis=0), end - gc))
        m = (jnp.where(upper, q, k) * e).astype(BF16)
        sb = lax.dot_general(m, m, nt, preferred_element_type=F32)
        scores = jnp.where(tx >= b, sb, scores)
        end = jnp.where(upper, end, pltpu.roll(end, n - b, axis=0))
        b *= 2
    scores = jnp.where((ti >= si) & (tx < CHUNK), scores, 0.0)

    intra = jnp.dot(scores.astype(BF16), vb, preferred_element_type=F32)
    o_s[rows, :] = inter + intra


def _hgrn_kernel(x_ref, nw_ref, win_ref, lb_ref, gn_ref, wout_ref, o_ref,
                 hn_s, proj_s, q_s, k_s, v_s, g_s, z_s, o_s, oh_s, upd_s, sb_s, st_ref):
    s = pl.program_id(1)
    h = pl.program_id(2)
    nh = pl.num_programs(2)
    ts = x_ref.shape[1]

    @pl.when(h == 0)
    def _():
        hn_s[...] = _rms(x_ref[0], nw_ref[...]).astype(BF16)
        proj_s[...] = jnp.dot(hn_s[...], win_ref[0], preferred_element_type=F32)

    @pl.when(s == 0)
    def _():
        st_ref[h] = jnp.zeros((HEAD_DIM, HEAD_DIM), F32)

    proj = proj_s[...]
    qr = proj[:, 0:HEAD_DIM]
    q_s[...] = qr * _sigmoid(qr)
    lb = lb_ref[0]
    forget = lb + (1.0 - lb) * _sigmoid(proj[:, HEAD_DIM:2 * HEAD_DIM])
    k_s[...] = 1.0 - forget
    v_s[...] = proj[:, 2 * HEAD_DIM:3 * HEAD_DIM]
    zr = proj[:, 3 * HEAD_DIM:4 * HEAD_DIM]
    z_s[...] = zr * _sigmoid(zr)

    g = jnp.log(forget)
    rmod = lax.broadcasted_iota(jnp.int32, (ts, 1), 0) % CHUNK
    dstep = 1
    while dstep < CHUNK:
        g = g + jnp.where(rmod >= dstep, pltpu.roll(g, dstep, axis=0), 0.0)
        dstep *= 2
    g_s[...] = g

    proj_s[...] = jnp.dot(hn_s[...], win_ref[jnp.minimum(h + 1, nh - 1)],
                          preferred_element_type=F32)

    nchunk = ts // CHUNK
    ngroup = ts // GROUP_ROWS
    for gi in range(ngroup):
        _hgrn_state_updates(gi, k_s, v_s, g_s, upd_s)

    st = st_ref[h]
    for c in range(nchunk):
        sb_s[c * HEAD_DIM:(c + 1) * HEAD_DIM, :] = st.astype(BF16)
        glast = g_s[(c + 1) * CHUNK - 1:(c + 1) * CHUNK, :]
        st = jnp.exp(glast) * st + upd_s[c]
    st_ref[h] = st

    for gi in range(ngroup):
        _hgrn_group(gi, q_s, k_s, v_s, g_s, o_s, sb_s)

    o = o_s[...]
    o = o * lax.rsqrt(jnp.mean(o * o, axis=-1, keepdims=True) + NORM_EPS) * gn_ref[...]
    oh_s[h] = (o * z_s[...]).astype(BF16)

    @pl.when(h == pl.num_programs(2) - 1)
    def _():
        oall = jnp.concatenate([oh_s[hh] for hh in range(HGRN_HEADS)], axis=1)
        o_ref[0] = x_ref[0] + jnp.dot(oall, wout_ref[...], preferred_element_type=F32)


def _hgrn_layer(h, nw, w_in, gn, w_out, lb):
    b, s, d = h.shape
    ts = HGRN_ROWS
    nh = HGRN_HEADS
    win_r = w_in.reshape(d, 4, nh, HEAD_DIM).transpose(2, 0, 1, 3).reshape(nh, d, 4 * HEAD_DIM)
    return pl.pallas_call(
        _hgrn_kernel,
        grid=(b, s // ts, nh),
        in_specs=[
            pl.BlockSpec((1, ts, d), lambda bi, si, hi: (bi, si, 0)),
            pl.BlockSpec((1, d), lambda bi, si, hi: (0, 0)),
            pl.BlockSpec((nh, d, 4 * HEAD_DIM), lambda bi, si, hi: (0, 0, 0),
                         pipeline_mode=pl.Buffered(1)),
            pl.BlockSpec((1, 1, HEAD_DIM), lambda bi, si, hi: (hi, 0, 0)),
            pl.BlockSpec((1, HEAD_DIM), lambda bi, si, hi: (0, 0)),
            pl.BlockSpec((d, d), lambda bi, si, hi: (0, 0), pipeline_mode=pl.Buffered(1)),
        ],
        out_specs=pl.BlockSpec((1, ts, d), lambda bi, si, hi: (bi, si, 0)),
        out_shape=jax.ShapeDtypeStruct((b, s, d), F32),
        scratch_shapes=[
            pltpu.VMEM((ts, d), BF16),
            pltpu.VMEM((ts, 4 * HEAD_DIM), F32),
            pltpu.VMEM((ts, HEAD_DIM), F32), pltpu.VMEM((ts, HEAD_DIM), F32),
            pltpu.VMEM((ts, HEAD_DIM), F32), pltpu.VMEM((ts, HEAD_DIM), F32),
            pltpu.VMEM((ts, HEAD_DIM), F32), pltpu.VMEM((ts, HEAD_DIM), F32),
            pltpu.VMEM((nh, ts, HEAD_DIM), BF16),
            pltpu.VMEM((ts // CHUNK, HEAD_DIM, HEAD_DIM), F32),
            pltpu.VMEM((ts // CHUNK * HEAD_DIM, HEAD_DIM), BF16),
            pltpu.VMEM((nh, HEAD_DIM, HEAD_DIM), F32),
        ],
        compiler_params=pltpu.CompilerParams(
            dimension_semantics=("arbitrary", "arbitrary", "arbitrary"),
            vmem_limit_bytes=VMEM_LIMIT),
        name="hgrn2",
    )(h, nw.reshape(1, d), win_r.astype(BF16), lb.reshape(nh, 1, HEAD_DIM),
      gn.reshape(1, HEAD_DIM), w_out.astype(BF16))


def _router_kernel(h_ref, nw_ref, wr_ref, br_ref, idx_ref, gate_ref, cnt_ref, cnt_s):
    i = pl.program_id(0)
    tm = h_ref.shape[0]

    @pl.when(i == 0)
    def _():
        cnt_s[...] = jnp.zeros_like(cnt_s)

    hn = _rms(h_ref[...], nw_ref[...])
    h1 = hn.astype(BF16)
    h2 = (hn - h1.astype(F32)).astype(BF16)
    w = wr_ref[...]
    w1 = w.astype(BF16)
    w2 = (w - w1.astype(F32)).astype(BF16)
    logits = (jnp.dot(h1, w1, preferred_element_type=F32) + jnp.dot(h1, w2, preferred_element_type=F32)
              + jnp.dot(h2, w1, preferred_element_type=F32)) + br_ref[...]
    li = lax.broadcasted_iota(jnp.int32, (tm, LANES), 1).astype(F32)
    neg = jnp.float32(-jnp.inf)
    big = jnp.float32(LANES)

    gl = jnp.where((li >= GROUP_LANE0) & (li < GROUP_LANE0 + N_GROUPS), logits, neg)
    gmax = jnp.max(gl, axis=-1, keepdims=True)
    gval = 1.0 / jnp.sum(jnp.exp(gl - gmax), axis=-1, keepdims=True)
    gidx = jnp.min(jnp.where(gl == gmax, li, big), axis=-1, keepdims=True) - GROUP_LANE0

    lo = gidx * EXPERTS_PER_GROUP
    el = jnp.where((li >= lo) & (li < lo + EXPERTS_PER_GROUP), logits, neg)
    v0 = jnp.max(el, axis=-1, keepdims=True)
    i0 = jnp.min(jnp.where(el == v0, li, big), axis=-1, keepdims=True)
    el2 = jnp.where(li == i0, neg, el)
    v1 = jnp.max(el2, axis=-1, keepdims=True)
    i1 = jnp.min(jnp.where(el2 == v1, li, big), axis=-1, keepdims=True)
    t = jnp.exp(v1 - v0)
    p0 = 1.0 / (1.0 + t)
    g0 = gval * p0
    g1 = gval * (t * p0)

    oh = jnp.where((li == i0) | (li == i1), 1.0, 0.0)
    rr = lax.broadcasted_iota(jnp.int32, (tm, tm), 0)
    cc = lax.broadcasted_iota(jnp.int32, (tm, tm), 1)
    lower = jnp.where(rr > cc, 1.0, 0.0).astype(BF16)
    before = jnp.dot(lower, oh.astype(BF16), preferred_element_type=F32) + cnt_s[...]
    r0 = jnp.sum(jnp.where(li == i0, before, 0.0), axis=-1, keepdims=True)
    r1 = jnp.sum(jnp.where(li == i1, before, 0.0), axis=-1, keepdims=True)
    cnt_s[...] = cnt_s[...] + jnp.sum(oh, axis=0, keepdims=True)

    rec = jnp.where(li == 0.0, i0, jnp.where(li == 1.0, i1,
                    jnp.where(li == 2.0, r0, jnp.where(li == 3.0, r1, 0.0))))
    idx_ref[...] = rec.T[0:8, :]
    gate_ref[...] = jnp.where(li == 0.0, g0, jnp.where(li == 1.0, g1, 0.0))
    cnt_ref[...] = cnt_s[...]


def _router(hf, nw, grp_w, grp_b, exp_w, exp_b):
    t, d = hf.shape
    tm = ROUTER_ROWS
    wr = jnp.zeros((d, LANES), F32)
    wr = wr.at[:, :N_EXPERTS].set(exp_w.reshape(d, N_EXPERTS))
    wr = wr.at[:, GROUP_LANE0:GROUP_LANE0 + N_GROUPS].set(grp_w)
    br = jnp.zeros((1, LANES), F32)
    br = br.at[0, :N_EXPERTS].set(exp_b.reshape(N_EXPERTS))
    br = br.at[0, GROUP_LANE0:GROUP_LANE0 + N_GROUPS].set(grp_b)
    return pl.pallas_call(
        _router_kernel,
        grid=(t // tm,),
        in_specs=[
            pl.BlockSpec((tm, d), lambda i: (i, 0)),
            pl.BlockSpec((1, d), lambda i: (0, 0)),
            pl.BlockSpec((d, LANES), lambda i: (0, 0)),
            pl.BlockSpec((1, LANES), lambda i: (0, 0)),
        ],
        out_specs=[
            pl.BlockSpec((8, tm), lambda i: (0, i)),
            pl.BlockSpec((tm, LANES), lambda i: (i, 0)),
            pl.BlockSpec((1, LANES), lambda i: (0, 0)),
        ],
        out_shape=[
            jax.ShapeDtypeStruct((8, t), F32),
            jax.ShapeDtypeStruct((t, LANES), F32),
            jax.ShapeDtypeStruct((1, LANES), F32),
        ],
        scratch_shapes=[pltpu.VMEM((1, LANES), F32)],
        compiler_params=pltpu.CompilerParams(
            dimension_semantics=("arbitrary",), vmem_limit_bytes=VMEM_LIMIT),
        name="moe_router",
    )(hf, nw.reshape(1, d), wr, br)


def _store_row_tiles(ref, x):
    n = x.shape[0]
    for j in range(SUBLANES):
        ref[pl.ds(j, n, stride=SUBLANES), :] = x[:, j * LANES:(j + 1) * LANES]


def _load_row_tiles(ref):
    n = ref.shape[0] // SUBLANES
    return jnp.concatenate([ref[pl.ds(j, n, stride=SUBLANES), :] for j in range(SUBLANES)], axis=1)


def _row_tile(ref, r):
    return ref.at[pl.ds(pl.multiple_of(r * SUBLANES, SUBLANES), SUBLANES)]


def _dispatch_kernel(d0_ref, d1_ref, zf_ref, h_ref, nw_ref, xs_hbm, stage, zbuf, sem, zsem):
    i = pl.program_id(0)
    nb = pl.num_programs(0)
    tm = h_ref.shape[0]
    blk = zbuf.shape[0]
    slot = i % 2

    def wait_slot(s):
        for _ in range(2):
            pltpu.make_async_copy(xs_hbm.at[pl.ds(0, tm)], xs_hbm.at[pl.ds(0, tm)], sem.at[s]).wait()

    @pl.when(i == 0)
    def _():
        zbuf[...] = jnp.zeros_like(zbuf)

        def zbody(e, carry):
            @pl.when(zf_ref[e] >= 0)
            def _():
                start = pl.multiple_of(zf_ref[e], blk)
                cp = pltpu.make_async_copy(zbuf, xs_hbm.at[pl.ds(start, blk)], zsem)
                cp.start()
                cp.wait()
            return carry
        lax.fori_loop(0, N_EXPERTS, zbody, 0)

        def tbody(b, carry):
            cp = pltpu.make_async_copy(zbuf, xs_hbm.at[pl.ds(pl.multiple_of(b * blk, blk), blk)], zsem)
            cp.start()
            cp.wait()
            return carry
        lax.fori_loop(zf_ref[N_EXPERTS], xs_hbm.shape[0] // blk, tbody, 0)

    @pl.when(i >= 2)
    def _():
        wait_slot(slot)

    _store_row_tiles(stage.at[slot], _rms(h_ref[...], nw_ref[...]))

    def body(rr, carry):
        for j in range(DMA_UNROLL):
            r = rr * DMA_UNROLL + j
            for kk, d_ref in enumerate((d0_ref, d1_ref)):
                pltpu.make_async_copy(_row_tile(stage.at[slot], r), xs_hbm.at[d_ref[i * tm + r]],
                                      sem.at[slot]).start(priority=kk)
        return carry
    lax.fori_loop(0, tm // DMA_UNROLL, body, 0)

    @pl.when(i == nb - 1)
    def _():
        wait_slot(slot)

        @pl.when(nb >= 2)
        def _():
            wait_slot(1 - slot)


def _dispatch(hf, nw, dest0, dest1, zfill, n_rows):
    t, d = hf.shape
    tm = DISPATCH_ROWS
    grid_spec = pltpu.PrefetchScalarGridSpec(
        num_scalar_prefetch=3,
        grid=(t // tm,),
        in_specs=[
            pl.BlockSpec((tm, d), lambda i, a, b, c: (i, 0)),
            pl.BlockSpec((1, d), lambda i, a, b, c: (0, 0)),
        ],
        out_specs=pl.BlockSpec(memory_space=pl.ANY),
        scratch_shapes=[pltpu.VMEM((2, tm * SUBLANES, d // SUBLANES), F32),
                        pltpu.VMEM((MOE_BLOCK, SUBLANES, d // SUBLANES), F32),
                        pltpu.SemaphoreType.DMA((2,)), pltpu.SemaphoreType.DMA(())],
    )
    return pl.pallas_call(
        _dispatch_kernel,
        grid_spec=grid_spec,
        out_shape=jax.ShapeDtypeStruct((n_rows, SUBLANES, d // SUBLANES), F32),
        compiler_params=pltpu.CompilerParams(
            dimension_semantics=("arbitrary",), vmem_limit_bytes=VMEM_LIMIT),
        name="moe_dispatch",
    )(dest0, dest1, zfill, hf, nw.reshape(1, d))


def _expert_kernel(be_ref, nu_ref, x_ref, wg_ref, wu_ref, wd_ref, y_ref, wg_s, wu_s, wd_s):
    i = pl.program_id(0)
    prev = be_ref[jnp.maximum(i - 1, 0)]

    @pl.when(i < nu_ref[0])
    def _():
        @pl.when((i == 0) | (be_ref[i] != prev))
        def _():
            wg_s[...] = wg_ref[0, 0].astype(BF16)
            wu_s[...] = wu_ref[0, 0].astype(BF16)
            wd_s[...] = wd_ref[0, 0].astype(BF16)

        hn = _load_row_tiles(x_ref).astype(BF16)
        g = jnp.dot(hn, wg_s[...], preferred_element_type=F32)
        u = jnp.dot(hn, wu_s[...], preferred_element_type=F32)
        a = (g * _sigmoid(g) * u).astype(BF16)
        _store_row_tiles(y_ref, jnp.dot(a, wd_s[...], preferred_element_type=F32))

    @pl.when(i >= nu_ref[0])
    def _():
        y_ref[...] = jnp.zeros_like(y_ref)


def _experts(xs, block_expert, n_used, w_gate, w_up, w_down, layer):
    n_rows = xs.shape[0]
    blk = MOE_BLOCK
    d, de = w_gate.shape[-2:]
    row_tile = (blk * SUBLANES, d // SUBLANES)
    used_blk = lambda i, be, nu: (jnp.minimum(i, nu[0] - 1), 0)
    grid_spec = pltpu.PrefetchScalarGridSpec(
        num_scalar_prefetch=2,
        grid=(n_rows // blk,),
        in_specs=[
            pl.BlockSpec(row_tile, used_blk),
            pl.BlockSpec((1, 1, d, de), lambda i, be, nu: (layer, be[i], 0, 0)),
            pl.BlockSpec((1, 1, d, de), lambda i, be, nu: (layer, be[i], 0, 0)),
            pl.BlockSpec((1, 1, de, d), lambda i, be, nu: (layer, be[i], 0, 0)),
        ],
        out_specs=pl.BlockSpec(row_tile, lambda i, be, nu: (i, 0)),
        scratch_shapes=[pltpu.VMEM((d, de), BF16), pltpu.VMEM((d, de), BF16),
                        pltpu.VMEM((de, d), BF16)],
    )
    return pl.pallas_call(
        _expert_kernel,
        grid_spec=grid_spec,
        out_shape=jax.ShapeDtypeStruct((n_rows * SUBLANES, d // SUBLANES), F32),
        compiler_params=pltpu.CompilerParams(
            dimension_semantics=("arbitrary",), vmem_limit_bytes=VMEM_LIMIT),
        name="moe_experts",
    )(block_expert, n_used, xs.reshape(n_rows * SUBLANES, d // SUBLANES), w_gate, w_up, w_down)


def _combine_kernel(d0_ref, d1_ref, y_hbm, h_ref, gate_ref, fw_ref, o_ref, ybuf, sem, *, final_norm):
    i = pl.program_id(0)
    nb = pl.num_programs(0)
    tc = h_ref.shape[0]

    def issue(b, slot):
        def body(rr, carry):
            for j in range(DMA_UNROLL):
                r = rr * DMA_UNROLL + j
                for kk, d_ref in enumerate((d0_ref, d1_ref)):
                    pltpu.make_async_copy(y_hbm.at[d_ref[b * tc + r]], _row_tile(ybuf.at[slot, kk], r),
                                          sem.at[slot]).start(priority=kk)
            return carry
        lax.fori_loop(0, tc // DMA_UNROLL, body, 0)

    @pl.when(i == 0)
    def _():
        issue(0, 0)

    @pl.when(i + 1 < nb)
    def _():
        issue(i + 1, (i + 1) % 2)

    slot = i % 2
    for kk in range(2):
        pltpu.make_async_copy(y_hbm.at[pl.ds(0, tc)], y_hbm.at[pl.ds(0, tc)], sem.at[slot]).wait()

    gate = gate_ref[...]
    y0 = _load_row_tiles(ybuf.at[slot, 0])
    y1 = _load_row_tiles(ybuf.at[slot, 1])
    o = h_ref[...] + (gate[:, 0:1] * y0 + gate[:, 1:2] * y1)
    if final_norm:
        o = _rms(o, fw_ref[...])
    o_ref[...] = o


def _combine(hf, y, dest0, dest1, gates, fw, final_norm):
    t, d = hf.shape
    tc = COMBINE_ROWS
    grid_spec = pltpu.PrefetchScalarGridSpec(
        num_scalar_prefetch=2,
        grid=(t // tc,),
        in_specs=[
            pl.BlockSpec(memory_space=pl.ANY),
            pl.BlockSpec((tc, d), lambda i, a, b: (i, 0)),
            pl.BlockSpec((tc, LANES), lambda i, a, b: (i, 0)),
            pl.BlockSpec((1, d), lambda i, a, b: (0, 0)),
        ],
        out_specs=pl.BlockSpec((tc, d), lambda i, a, b: (i, 0)),
        scratch_shapes=[pltpu.VMEM((2, 2, tc * SUBLANES, d // SUBLANES), F32),
                        pltpu.SemaphoreType.DMA((2,))],
    )
    return pl.pallas_call(
        functools.partial(_combine_kernel, final_norm=final_norm),
        grid_spec=grid_spec,
        out_shape=jax.ShapeDtypeStruct((t, d), F32),
        compiler_params=pltpu.CompilerParams(
            dimension_semantics=("arbitrary",), vmem_limit_bytes=VMEM_LIMIT),
        name="moe_combine",
    )(dest0, dest1, y.reshape(-1, SUBLANES, d // SUBLANES), hf, gates, fw.reshape(1, d))


def _moe_layer(h, nw, grp_w, grp_b, exp_w, exp_b, w_gate, w_up, w_down, fw, layer, final_norm):
    b, s, d = h.shape
    t = b * s
    hf = h.reshape(t, d)
    rec, gates, cnt = _router(hf, nw, grp_w, grp_b, exp_w, exp_b)

    rec = rec.astype(jnp.int32)
    counts = cnt[0, :N_EXPERTS].astype(jnp.int32)
    padded = (counts + MOE_BLOCK - 1) // MOE_BLOCK * MOE_BLOCK
    pad_end = jnp.cumsum(padded)
    pad_start = pad_end - padded
    eids = jnp.arange(N_EXPERTS, dtype=jnp.int32)[:, None]
    dest0 = jnp.sum(jnp.where(rec[0][None, :] == eids, pad_start[:, None], 0), axis=0) + rec[2]
    dest1 = jnp.sum(jnp.where(rec[1][None, :] == eids, pad_start[:, None], 0), axis=0) + rec[3]
    n_rows = 2 * t + N_EXPERTS * MOE_BLOCK
    n_blocks = n_rows // MOE_BLOCK
    block_start = jnp.arange(n_blocks, dtype=jnp.int32) * MOE_BLOCK
    block_expert = jnp.minimum(
        jnp.sum((block_start[:, None] >= pad_end[None, :]).astype(jnp.int32), axis=1), N_EXPERTS - 1)
    n_used = (pad_end[-1:] // MOE_BLOCK).astype(jnp.int32)
    zfill = jnp.concatenate([jnp.where(padded > 0, pad_end - MOE_BLOCK, -1).astype(jnp.int32), n_used])

    xs = _dispatch(hf, nw, dest0, dest1, zfill, n_rows)
    y = _experts(xs, block_expert, n_used, w_gate, w_up, w_down, layer)
    out = _combine(hf, y, dest0, dest1, gates, fw, final_norm)
    return out.reshape(b, s, d)


def kernel(x, conv_norm_w, conv_pw1_w, conv_pw1_b, conv_dw_w, conv_dw_b, conv_ln_g, conv_ln_b,
           conv_pw2_w, conv_pw2_b, hgrn_norm_w, hgrn_w_in, hgrn_gnorm_w, hgrn_w_out, lower_bounds,
           ffn_norm_w, router_grp_w, router_grp_b, router_exp_w, router_exp_b,
           moe_w_gate, moe_w_up, moe_w_down, final_norm_w):
    depth = lower_bounds.shape[0]
    lb_p = jax.nn.softmax(lower_bounds.astype(F32), axis=0)
    lb_all = jnp.cumsum(lb_p, axis=0) - lb_p[0]
    h = x
    for layer in range(depth):
        j = layer // 2
        if layer % 2 == 0:
            h = _conformer_layer(h, conv_norm_w[j], conv_pw1_w[j], conv_pw1_b[j], conv_dw_w[j],
                                 conv_dw_b[j], conv_ln_g[j], conv_ln_b[j], conv_pw2_w[j],
                                 conv_pw2_b[j])
        else:
            h = _hgrn_layer(h, hgrn_norm_w[j], hgrn_w_in[j], hgrn_gnorm_w[j], hgrn_w_out[j],
                            lb_all[layer])
        h = _moe_layer(h, ffn_norm_w[layer], router_grp_w[layer], router_grp_b[layer],
                       router_exp_w[layer], router_exp_b[layer], moe_w_gate, moe_w_up,
                       moe_w_down, final_norm_w, layer=layer, final_norm=(layer == depth - 1))
    return h
```

```python
import functools

import jax
import jax.numpy as jnp
from jax import lax
from jax.experimental import pallas as pl
from jax.experimental.pallas import tpu as pltpu

F32 = jnp.float32
BF16 = jnp.bfloat16

NORM_EPS = 1e-6
CONV_WIDTH = 31
CONV_HALO = 32
HGRN_HEADS = 8
HEAD_DIM = 128
CHUNK = 64
N_GROUPS = 4
EXPERTS_PER_GROUP = 8
N_EXPERTS = N_GROUPS * EXPERTS_PER_GROUP
LANES = 128
SUBLANES = 8
GROUP_LANE0 = N_EXPERTS

CONV_ROWS = 512
CONV_RC = 128
HGRN_ROWS = 1024
ROUTER_ROWS = 1024
MOE_BLOCK = 512
DMA_UNROLL = 8
DISPATCH_ROWS = 512
COMBINE_ROWS = 512
VMEM_LIMIT = 48 * 1024 * 1024


def _rms(x, w):
    ms = jnp.mean(x * x, axis=-1, keepdims=True)
    return x * lax.rsqrt(ms + NORM_EPS) * w


def _sigmoid(x):
    return 1.0 / (1.0 + jnp.exp(-x))


def _conv_kernel(x_ref, nw_ref, pw1_ref, b1_ref, dw_ref, dwb_ref, lng_ref, lnb_ref,
                 pw2_ref, b2_ref, o_ref, ubuf, cbuf):
    s = pl.program_id(1)
    ts = x_ref.shape[1]
    d = x_ref.shape[2]
    x = x_ref[0]
    hn = _rms(x, nw_ref[...]).astype(BF16)
    a = jnp.dot(hn, pw1_ref[...], preferred_element_type=F32) + b1_ref[...]
    u = a[:, :d] * _sigmoid(a[:, d:])

    @pl.when(s == 0)
    def _():
        ubuf[0:CONV_HALO, :] = jnp.zeros((CONV_HALO, d), F32)

    @pl.when(s > 0)
    def _():
        ubuf[0:CONV_HALO, :] = ubuf[ts:ts + CONV_HALO, :]

    ubuf[CONV_HALO:, :] = u

    off0 = CONV_HALO - (CONV_WIDTH - 1)
    win = CONV_RC + CONV_HALO
    for rc in range(ts // CONV_RC):
        for lc in range(d // LANES):
            ls = slice(lc * LANES, (lc + 1) * LANES)
            xw = ubuf[rc * CONV_RC:rc * CONV_RC + win, ls]
            acc = jnp.broadcast_to(dwb_ref[:, ls], (CONV_RC, LANES))
            for p in range(SUBLANES):
                xp = xw if p == 0 else pltpu.roll(xw, win - p, axis=0)
                for j in range(CONV_WIDTH):
                    if (off0 + j) % SUBLANES == p:
                        a0 = off0 + j - p
                        acc = acc + dw_ref[j:j + 1, ls] * xp[a0:a0 + CONV_RC, :]
            cbuf[rc * CONV_RC:(rc + 1) * CONV_RC, ls] = acc

    c = cbuf[...]
    mu = jnp.mean(c, axis=-1, keepdims=True)
    cc = c - mu
    var = jnp.mean(cc * cc, axis=-1, keepdims=True)
    n = cc * lax.rsqrt(var + NORM_EPS) * lng_ref[...] + lnb_ref[...]
    sw = (n * _sigmoid(n)).astype(BF16)
    y = jnp.dot(sw, pw2_ref[...], preferred_element_type=F32) + b2_ref[...]
    o_ref[0] = x + y


def _conformer_layer(h, nw, pw1, b1, dw, dwb, lng, lnb, pw2, b2):
    b, s, d = h.shape
    ts = CONV_ROWS
    row = lambda v: v.reshape(1, -1)
    full = lambda shape: pl.BlockSpec(shape, lambda bi, si: (0,) * len(shape))
    return pl.pallas_call(
        _conv_kernel,
        grid=(b, s // ts),
        in_specs=[
            pl.BlockSpec((1, ts, d), lambda bi, si: (bi, si, 0)),
            full((1, d)), full((d, 2 * d)), full((1, 2 * d)),
            full((CONV_WIDTH, d)), full((1, d)), full((1, d)), full((1, d)),
            full((d, d)), full((1, d)),
        ],
        out_specs=pl.BlockSpec((1, ts, d), lambda bi, si: (bi, si, 0)),
        out_shape=jax.ShapeDtypeStruct((b, s, d), F32),
        scratch_shapes=[pltpu.VMEM((ts + CONV_HALO, d), F32), pltpu.VMEM((ts, d), F32)],
        compiler_params=pltpu.CompilerParams(
            dimension_semantics=("arbitrary", "arbitrary"), vmem_limit_bytes=VMEM_LIMIT),
        name="conformer_conv",
    )(h, row(nw), pw1.astype(BF16), row(b1), dw, row(dwb), row(lng), row(lnb),
      pw2.astype(BF16), row(b2))


HGRN_GROUP = 4
GROUP_ROWS = HGRN_GROUP * CHUNK


def _hgrn_state_updates(gi, k_s, v_s, g_s, upd_s):
    rows = pl.ds(pl.multiple_of(gi * GROUP_ROWS, GROUP_ROWS), GROUP_ROWS)
    gc = g_s[rows, :]
    k = k_s[rows, :]
    chunk_of_row = lax.broadcasted_iota(jnp.int32, (GROUP_ROWS, HEAD_DIM), 0) // CHUNK
    glast = jnp.concatenate(
        [jnp.broadcast_to(gc[(j + 1) * CHUNK - 1:(j + 1) * CHUNK, :], (CHUNK, HEAD_DIM))
         for j in range(HGRN_GROUP)], axis=0)
    kd = k * jnp.exp(glast - gc)
    rhs = jnp.concatenate([jnp.where(chunk_of_row == j, kd, 0.0).astype(BF16)
                           for j in range(HGRN_GROUP)], axis=1)
    vb = v_s[rows, :].astype(BF16)
    upd = lax.dot_general(vb, rhs, (((0,), (0,)), ((), ())), preferred_element_type=F32)
    for j in range(HGRN_GROUP):
        upd_s[gi * HGRN_GROUP + j] = upd[:, j * HEAD_DIM:(j + 1) * HEAD_DIM]


def _hgrn_group(gi, q_s, k_s, v_s, g_s, o_s, sb_s):
    n = GROUP_ROWS
    rows = pl.ds(pl.multiple_of(gi * n, n), n)
    q = q_s[rows, :]
    k = k_s[rows, :]
    gc = g_s[rows, :]
    vb = v_s[rows, :].astype(BF16)
    nt = (((1,), (1,)), ((), ()))

    qg = (q * jnp.exp(gc)).astype(BF16)
    sts = sb_s[pl.ds(pl.multiple_of(gi * HGRN_GROUP * HEAD_DIM, HGRN_GROUP * HEAD_DIM),
                     HGRN_GROUP * HEAD_DIM), :]
    wide = lax.dot_general(qg, sts, nt, preferred_element_type=F32)
    inter = jnp.concatenate(
        [wide[j * CHUNK:(j + 1) * CHUNK, j * HEAD_DIM:(j + 1) * HEAD_DIM]
         for j in range(HGRN_GROUP)], axis=0)

    row = lax.broadcasted_iota(jnp.int32, (n, HEAD_DIM), 0)
    ti = lax.broadcasted_iota(jnp.int32, (n, n), 0)
    si = lax.broadcasted_iota(jnp.int32, (n, n), 1)
    tx = ti ^ si
    scores = jnp.broadcast_to(jnp.sum(q * k, axis=-1, keepdims=True), (n, n))
    end = gc
    b = 1
    while b < CHUNK:
        upper = (row & b) != 0
        e = jnp.exp(jnp.where(upper, gc - pltpu.roll(end, b, axis=0), end - gc))
        m = (jnp.where(upper, q, k) * e).astype(BF16)
        sb = lax.dot_general(m, m, nt, preferred_element_type=F32)
        scores = jnp.where(tx >= b, sb, scores)
        end = jnp.where(upper, end, pltpu.roll(end, n - b, axis=0))
        b *= 2
    scores = jnp.where((ti >= si) & (tx < CHUNK), scores, 0.0)

    intra = jnp.dot(scores.astype(BF16), vb, preferred_element_type=F32)
    o_s[rows, :] = inter + intra


def _hgrn_kernel(x_ref, nw_ref, win_ref, lb_ref, gn_ref, wout_ref, o_ref,
                 hn_s, proj_s, q_s, k_s, v_s, g_s, z_s, o_s, oh_s, upd_s, sb_s, st_ref):
    s = pl.program_id(1)
    h = pl.program_id(2)
    nh = pl.num_programs(2)
    ts = x_ref.shape[1]

    @pl.when(h == 0)
    def _():
        hn_s[...] = _rms(x_ref[0], nw_ref[...]).astype(BF16)
        proj_s[...] = jnp.dot(hn_s[...], win_ref[0], preferred_element_type=F32)

    @pl.when(s == 0)
    def _():
        st_ref[h] = jnp.zeros((HEAD_DIM, HEAD_DIM), F32)

    proj = proj_s[...]
    qr = proj[:, 0:HEAD_DIM]
    q_s[...] = qr * _sigmoid(qr)
    lb = lb_ref[0]
    forget = lb + (1.0 - lb) * _sigmoid(proj[:, HEAD_DIM:2 * HEAD_DIM])
    k_s[...] = 1.0 - forget
    v_s[...] = proj[:, 2 * HEAD_DIM:3 * HEAD_DIM]
    zr = proj[:, 3 * HEAD_DIM:4 * HEAD_DIM]
    z_s[...] = zr * _sigmoid(zr)

    g = jnp.log(forget)
    rmod = lax.broadcasted_iota(jnp.int32, (ts, 1), 0) % CHUNK
    dstep = 1
    while dstep < CHUNK:
        g = g + jnp.where(rmod >= dstep, pltpu.roll(g, dstep, axis=0), 0.0)
        dstep *= 2
    g_s[...] = g

    proj_s[...] = jnp.dot(hn_s[...], win_ref[jnp.minimum(h + 1, nh - 1)],
                          preferred_element_type=F32)

    nchunk = ts // CHUNK
    ngroup = ts // GROUP_ROWS
    for gi in range(ngroup):
        _hgrn_state_updates(gi, k_s, v_s, g_s, upd_s)

    st = st_ref[h]
    for c in range(nchunk):
        sb_s[c * HEAD_DIM:(c + 1) * HEAD_DIM, :] = st.astype(BF16)
        glast = g_s[(c + 1) * CHUNK - 1:(c + 1) * CHUNK, :]
        st = jnp.exp(glast) * st + upd_s[c]
    st_ref[h] = st

    for gi in range(ngroup):
        _hgrn_group(gi, q_s, k_s, v_s, g_s, o_s, sb_s)

    o = o_s[...]
    o = o * lax.rsqrt(jnp.mean(o * o, axis=-1, keepdims=True) + NORM_EPS) * gn_ref[...]
    oh_s[h] = (o * z_s[...]).astype(BF16)

    @pl.when(h == pl.num_programs(2) - 1)
    def _():
        oall = jnp.concatenate([oh_s[hh] for hh in range(HGRN_HEADS)], axis=1)
        o_ref[0] = x_ref[0] + jnp.dot(oall, wout_ref[...], preferred_element_type=F32)


def _hgrn_layer(h, nw, w_in, gn, w_out, lb):
    b, s, d = h.shape
    ts = HGRN_ROWS
    nh = HGRN_HEADS
    win_r = w_in.astype(BF16).reshape(d, 4, nh, HEAD_DIM).transpose(2, 0, 1, 3).reshape(
        nh, d, 4 * HEAD_DIM)
    return pl.pallas_call(
        _hgrn_kernel,
        grid=(b, s // ts, nh),
        in_specs=[
            pl.BlockSpec((1, ts, d), lambda bi, si, hi: (bi, si, 0)),
            pl.BlockSpec((1, d), lambda bi, si, hi: (0, 0)),
            pl.BlockSpec((nh, d, 4 * HEAD_DIM), lambda bi, si, hi: (0, 0, 0),
                         pipeline_mode=pl.Buffered(1)),
            pl.BlockSpec((1, 1, HEAD_DIM), lambda bi, si, hi: (hi, 0, 0)),
            pl.BlockSpec((1, HEAD_DIM), lambda bi, si, hi: (0, 0)),
            pl.BlockSpec((d, d), lambda bi, si, hi: (0, 0), pipeline_mode=pl.Buffered(1)),
        ],
        out_specs=pl.BlockSpec((1, ts, d), lambda bi, si, hi: (bi, si, 0)),
        out_shape=jax.ShapeDtypeStruct((b, s, d), F32),
        scratch_shapes=[
            pltpu.VMEM((ts, d), BF16),
            pltpu.VMEM((ts, 4 * HEAD_DIM), F32),
            pltpu.VMEM((ts, HEAD_DIM), F32), pltpu.VMEM((ts, HEAD_DIM), F32),
            pltpu.VMEM((ts, HEAD_DIM), F32), pltpu.VMEM((ts, HEAD_DIM), F32),
            pltpu.VMEM((ts, HEAD_DIM), F32), pltpu.VMEM((ts, HEAD_DIM), F32),
            pltpu.VMEM((nh, ts, HEAD_DIM), BF16),
            pltpu.VMEM((ts // CHUNK, HEAD_DIM, HEAD_DIM), F32),
            pltpu.VMEM((ts // CHUNK * HEAD_DIM, HEAD_DIM), BF16),
            pltpu.VMEM((nh, HEAD_DIM, HEAD_DIM), F32),
        ],
        compiler_params=pltpu.CompilerParams(
            dimension_semantics=("arbitrary", "arbitrary", "arbitrary"),
            vmem_limit_bytes=VMEM_LIMIT),
        name="hgrn2",
    )(h, nw.reshape(1, d), win_r, lb.reshape(nh, 1, HEAD_DIM),
      gn.reshape(1, HEAD_DIM), w_out.astype(BF16))


def _router_kernel(h_ref, nw_ref, wr_ref, br_ref, idx_ref, gate_ref, cnt_ref, cnt_s):
    i = pl.program_id(0)
    tm = h_ref.shape[0]

    @pl.when(i == 0)
    def _():
        cnt_s[...] = jnp.zeros_like(cnt_s)

    hn = _rms(h_ref[...], nw_ref[...])
    h1 = hn.astype(BF16)
    h2 = (hn - h1.astype(F32)).astype(BF16)
    w = wr_ref[...]
    w1 = w.astype(BF16)
    w2 = (w - w1.astype(F32)).astype(BF16)
    logits = (jnp.dot(h1, w1, preferred_element_type=F32) + jnp.dot(h1, w2, preferred_element_type=F32)
              + jnp.dot(h2, w1, preferred_element_type=F32)) + br_ref[...]
    li = lax.broadcasted_iota(jnp.int32, (tm, LANES), 1).astype(F32)
    neg = jnp.float32(-jnp.inf)
    big = jnp.float32(LANES)

    gl = jnp.where((li >= GROUP_LANE0) & (li < GROUP_LANE0 + N_GROUPS), logits, neg)
    gmax = jnp.max(gl, axis=-1, keepdims=True)
    gval = 1.0 / jnp.sum(jnp.exp(gl - gmax), axis=-1, keepdims=True)
    gidx = jnp.min(jnp.where(gl == gmax, li, big), axis=-1, keepdims=True) - GROUP_LANE0

    lo = gidx * EXPERTS_PER_GROUP
    el = jnp.where((li >= lo) & (li < lo + EXPERTS_PER_GROUP), logits, neg)
    v0 = jnp.max(el, axis=-1, keepdims=True)
    i0 = jnp.min(jnp.where(el == v0, li, big), axis=-1, keepdims=True)
    el2 = jnp.where(li == i0, neg, el)
    v1 = jnp.max(el2, axis=-1, keepdims=True)
    i1 = jnp.min(jnp.where(el2 == v1, li, big), axis=-1, keepdims=True)
    t = jnp.exp(v1 - v0)
    p0 = 1.0 / (1.0 + t)
    g0 = gval * p0
    g1 = gval * (t * p0)

    oh = jnp.where((li == i0) | (li == i1), 1.0, 0.0)
    rr = lax.broadcasted_iota(jnp.int32, (tm, tm), 0)
    cc = lax.broadcasted_iota(jnp.int32, (tm, tm), 1)
    lower = jnp.where(rr > cc, 1.0, 0.0).astype(BF16)
    before = jnp.dot(lower, oh.astype(BF16), preferred_element_type=F32) + cnt_s[...]
    r0 = jnp.sum(jnp.where(li == i0, before, 0.0), axis=-1, keepdims=True)
    r1 = jnp.sum(jnp.where(li == i1, before, 0.0), axis=-1, keepdims=True)
    cnt_s[...] = cnt_s[...] + jnp.sum(oh, axis=0, keepdims=True)

    rec = jnp.where(li == 0.0, i0, jnp.where(li == 1.0, i1,
                    jnp.where(li == 2.0, r0, jnp.where(li == 3.0, r1, 0.0))))
    idx_ref[...] = rec.T[0:8, :]
    gate_ref[...] = jnp.where(li == 0.0, g0, jnp.where(li == 1.0, g1, 0.0))
    cnt_ref[...] = cnt_s[...]


def _router(hf, nw, grp_w, grp_b, exp_w, exp_b):
    t, d = hf.shape
    tm = ROUTER_ROWS
    wr = jnp.zeros((d, LANES), F32)
    wr = wr.at[:, :N_EXPERTS].set(exp_w.reshape(d, N_EXPERTS))
    wr = wr.at[:, GROUP_LANE0:GROUP_LANE0 + N_GROUPS].set(grp_w)
    br = jnp.zeros((1, LANES), F32)
    br = br.at[0, :N_EXPERTS].set(exp_b.reshape(N_EXPERTS))
    br = br.at[0, GROUP_LANE0:GROUP_LANE0 + N_GROUPS].set(grp_b)
    return pl.pallas_call(
        _router_kernel,
        grid=(t // tm,),
        in_specs=[
            pl.BlockSpec((tm, d), lambda i: (i, 0)),
            pl.BlockSpec((1, d), lambda i: (0, 0)),
            pl.BlockSpec((d, LANES), lambda i: (0, 0)),
            pl.BlockSpec((1, LANES), lambda i: (0, 0)),
        ],
        out_specs=[
            pl.BlockSpec((8, tm), lambda i: (0, i)),
            pl.BlockSpec((tm, LANES), lambda i: (i, 0)),
            pl.BlockSpec((1, LANES), lambda i: (0, 0)),
        ],
        out_shape=[
            jax.ShapeDtypeStruct((8, t), F32),
            jax.ShapeDtypeStruct((t, LANES), F32),
            jax.ShapeDtypeStruct((1, LANES), F32),
        ],
        scratch_shapes=[pltpu.VMEM((1, LANES), F32)],
        compiler_params=pltpu.CompilerParams(
            dimension_semantics=("arbitrary",), vmem_limit_bytes=VMEM_LIMIT),
        name="moe_router",
    )(hf, nw.reshape(1, d), wr, br)


def _store_row_tiles(ref, x):
    n = x.shape[0]
    for j in range(SUBLANES):
        ref[pl.ds(j, n, stride=SUBLANES), :] = x[:, j * LANES:(j + 1) * LANES]


def _load_row_tiles(ref):
    n = ref.shape[0] // SUBLANES
    return jnp.concatenate([ref[pl.ds(j, n, stride=SUBLANES), :] for j in range(SUBLANES)], axis=1)


def _row_tile(ref, r):
    return ref.at[pl.ds(pl.multiple_of(r * SUBLANES, SUBLANES), SUBLANES)]


def _dispatch_kernel(d0_ref, d1_ref, zf_ref, h_ref, nw_ref, xs_hbm, stage, zbuf, sem, zsem):
    i = pl.program_id(0)
    nb = pl.num_programs(0)
    tm = h_ref.shape[0]
    blk = zbuf.shape[0]
    slot = i % 2

    def wait_slot(s):
        for _ in range(2):
            pltpu.make_async_copy(xs_hbm.at[pl.ds(0, tm)], xs_hbm.at[pl.ds(0, tm)], sem.at[s]).wait()

    @pl.when(i == 0)
    def _():
        zbuf[...] = jnp.zeros_like(zbuf)

        def zbody(e, carry):
            @pl.when(zf_ref[e] >= 0)
            def _():
                start = pl.multiple_of(zf_ref[e], blk)
                cp = pltpu.make_async_copy(zbuf, xs_hbm.at[pl.ds(start, blk)], zsem)
                cp.start()
                cp.wait()
            return carry
        lax.fori_loop(0, N_EXPERTS, zbody, 0)

        def tbody(b, carry):
            cp = pltpu.make_async_copy(zbuf, xs_hbm.at[pl.ds(pl.multiple_of(b * blk, blk), blk)], zsem)
            cp.start()
            cp.wait()
            return carry
        lax.fori_loop(zf_ref[N_EXPERTS], xs_hbm.shape[0] // blk, tbody, 0)

    @pl.when(i >= 2)
    def _():
        wait_slot(slot)

    _store_row_tiles(stage.at[slot], _rms(h_ref[...], nw_ref[...]))

    def body(rr, carry):
        for j in range(DMA_UNROLL):
            r = rr * DMA_UNROLL + j
            for kk, d_ref in enumerate((d0_ref, d1_ref)):
                pltpu.make_async_copy(_row_tile(stage.at[slot], r), xs_hbm.at[d_ref[i * tm + r]],
                                      sem.at[slot]).start(priority=kk)
        return carry
    lax.fori_loop(0, tm // DMA_UNROLL, body, 0)

    @pl.when(i == nb - 1)
    def _():
        wait_slot(slot)

        @pl.when(nb >= 2)
        def _():
            wait_slot(1 - slot)


def _dispatch(hf, nw, dest0, dest1, zfill, n_rows):
    t, d = hf.shape
    tm = DISPATCH_ROWS
    grid_spec = pltpu.PrefetchScalarGridSpec(
        num_scalar_prefetch=3,
        grid=(t // tm,),
        in_specs=[
            pl.BlockSpec((tm, d), lambda i, a, b, c: (i, 0)),
            pl.BlockSpec((1, d), lambda i, a, b, c: (0, 0)),
        ],
        out_specs=pl.BlockSpec(memory_space=pl.ANY),
        scratch_shapes=[pltpu.VMEM((2, tm * SUBLANES, d // SUBLANES), F32),
                        pltpu.VMEM((MOE_BLOCK, SUBLANES, d // SUBLANES), F32),
                        pltpu.SemaphoreType.DMA((2,)), pltpu.SemaphoreType.DMA(())],
    )
    return pl.pallas_call(
        _dispatch_kernel,
        grid_spec=grid_spec,
        out_shape=jax.ShapeDtypeStruct((n_rows, SUBLANES, d // SUBLANES), F32),
        compiler_params=pltpu.CompilerParams(
            dimension_semantics=("arbitrary",), vmem_limit_bytes=VMEM_LIMIT),
        name="moe_dispatch",
    )(dest0, dest1, zfill, hf, nw.reshape(1, d))


def _expert_kernel(be_ref, nu_ref, x_ref, wg_ref, wu_ref, wd_ref, y_ref, wg_s, wu_s, wd_s):
    i = pl.program_id(0)
    prev = be_ref[jnp.maximum(i - 1, 0)]

    @pl.when(i < nu_ref[0])
    def _():
        @pl.when((i == 0) | (be_ref[i] != prev))
        def _():
            wg_s[...] = wg_ref[0, 0].astype(BF16)
            wu_s[...] = wu_ref[0, 0].astype(BF16)
            wd_s[...] = wd_ref[0, 0].astype(BF16)

        hn = _load_row_tiles(x_ref).astype(BF16)
        g = jnp.dot(hn, wg_s[...], preferred_element_type=F32)
        u = jnp.dot(hn, wu_s[...], preferred_element_type=F32)
        a = (g * _sigmoid(g) * u).astype(BF16)
        _store_row_tiles(y_ref, jnp.dot(a, wd_s[...], preferred_element_type=F32))

    @pl.when(i >= nu_ref[0])
    def _():
        y_ref[...] = jnp.zeros_like(y_ref)


def _experts(xs, block_expert, n_used, w_gate, w_up, w_down, layer):
    n_rows = xs.shape[0]
    blk = MOE_BLOCK
    d, de = w_gate.shape[-2:]
    row_tile = (blk * SUBLANES, d // SUBLANES)
    used_blk = lambda i, be, nu: (jnp.minimum(i, nu[0] - 1), 0)
    grid_spec = pltpu.PrefetchScalarGridSpec(
        num_scalar_prefetch=2,
        grid=(n_rows // blk,),
        in_specs=[
            pl.BlockSpec(row_tile, used_blk),
            pl.BlockSpec((1, 1, d, de), lambda i, be, nu: (layer, be[i], 0, 0)),
            pl.BlockSpec((1, 1, d, de), lambda i, be, nu: (layer, be[i], 0, 0)),
            pl.BlockSpec((1, 1, de, d), lambda i, be, nu: (layer, be[i], 0, 0)),
        ],
        out_specs=pl.BlockSpec(row_tile, lambda i, be, nu: (i, 0)),
        scratch_shapes=[pltpu.VMEM((d, de), BF16), pltpu.VMEM((d, de), BF16),
                        pltpu.VMEM((de, d), BF16)],
    )
    return pl.pallas_call(
        _expert_kernel,
        grid_spec=grid_spec,
        out_shape=jax.ShapeDtypeStruct((n_rows * SUBLANES, d // SUBLANES), F32),
        compiler_params=pltpu.CompilerParams(
            dimension_semantics=("arbitrary",), vmem_limit_bytes=VMEM_LIMIT),
        name="moe_experts",
    )(block_expert, n_used, xs.reshape(n_rows * SUBLANES, d // SUBLANES), w_gate, w_up, w_down)


def _combine_kernel(d0_ref, d1_ref, y_hbm, h_ref, gate_ref, fw_ref, o_ref, ybuf, sem, *, final_norm):
    i = pl.program_id(0)
    nb = pl.num_programs(0)
    tc = h_ref.shape[0]

    def issue(b, slot):
        def body(rr, carry):
            for j in range(DMA_UNROLL):
                r = rr * DMA_UNROLL + j
                for kk, d_ref in enumerate((d0_ref, d1_ref)):
                    pltpu.make_async_copy(y_hbm.at[d_ref[b * tc + r]], _row_tile(ybuf.at[slot, kk], r),
                                          sem.at[slot]).start(priority=kk)
            return carry
        lax.fori_loop(0, tc // DMA_UNROLL, body, 0)

    @pl.when(i == 0)
    def _():
        issue(0, 0)

    @pl.when(i + 1 < nb)
    def _():
        issue(i + 1, (i + 1) % 2)

    slot = i % 2
    for kk in range(2):
        pltpu.make_async_copy(y_hbm.at[pl.ds(0, tc)], y_hbm.at[pl.ds(0, tc)], sem.at[slot]).wait()

    gate = gate_ref[...]
    y0 = _load_row_tiles(ybuf.at[slot, 0])
    y1 = _load_row_tiles(ybuf.at[slot, 1])
    o = h_ref[...] + (gate[:, 0:1] * y0 + gate[:, 1:2] * y1)
    if final_norm:
        o = _rms(o, fw_ref[...])
    o_ref[...] = o


def _combine(hf, y, dest0, dest1, gates, fw, final_norm):
    t, d = hf.shape
    tc = COMBINE_ROWS
    grid_spec = pltpu.PrefetchScalarGridSpec(
        num_scalar_prefetch=2,
        grid=(t // tc,),
        in_specs=[
            pl.BlockSpec(memory_space=pl.ANY),
            pl.BlockSpec((tc, d), lambda i, a, b: (i, 0)),
            pl.BlockSpec((tc, LANES), lambda i, a, b: (i, 0)),
            pl.BlockSpec((1, d), lambda i, a, b: (0, 0)),
        ],
        out_specs=pl.BlockSpec((tc, d), lambda i, a, b: (i, 0)),
        scratch_shapes=[pltpu.VMEM((2, 2, tc * SUBLANES, d // SUBLANES), F32),
                        pltpu.SemaphoreType.DMA((2,))],
    )
    return pl.pallas_call(
        functools.partial(_combine_kernel, final_norm=final_norm),
        grid_spec=grid_spec,
        out_shape=jax.ShapeDtypeStruct((t, d), F32),
        compiler_params=pltpu.CompilerParams(
            dimension_semantics=("arbitrary",), vmem_limit_bytes=VMEM_LIMIT),
        name="moe_combine",
    )(dest0, dest1, y.reshape(-1, SUBLANES, d // SUBLANES), hf, gates, fw.reshape(1, d))


def _moe_layer(h, nw, grp_w, grp_b, exp_w, exp_b, w_gate, w_up, w_down, fw, layer, final_norm):
    b, s, d = h.shape
    t = b * s
    hf = h.reshape(t, d)
    rec, gates, cnt = _router(hf, nw, grp_w, grp_b, exp_w, exp_b)

    rec = rec.astype(jnp.int32)
    counts = cnt[0, :N_EXPERTS].astype(jnp.int32)
    padded = (counts + MOE_BLOCK - 1) // MOE_BLOCK * MOE_BLOCK
    pad_end = jnp.cumsum(padded)
    pad_start = pad_end - padded
    eids = jnp.arange(N_EXPERTS, dtype=jnp.int32)[:, None]
    dest0 = jnp.sum(jnp.where(rec[0][None, :] == eids, pad_start[:, None], 0), axis=0) + rec[2]
    dest1 = jnp.sum(jnp.where(rec[1][None, :] == eids, pad_start[:, None], 0), axis=0) + rec[3]
    n_rows = 2 * t + N_EXPERTS * MOE_BLOCK
    n_blocks = n_rows // MOE_BLOCK
    block_start = jnp.arange(n_blocks, dtype=jnp.int32) * MOE_BLOCK
    block_expert = jnp.minimum(
        jnp.sum((block_start[:, None] >= pad_end[None, :]).astype(jnp.int32), axis=1), N_EXPERTS - 1)
    n_used = (pad_end[-1:] // MOE_BLOCK).astype(jnp.int32)
    zfill = jnp.concatenate([jnp.where(padded > 0, pad_end - MOE_BLOCK, -1).astype(jnp.int32), n_used])

    xs = _dispatch(hf, nw, dest0, dest1, zfill, n_rows)
    y = _experts(xs, block_expert, n_used, w_gate, w_up, w_down, layer)
    out = _combine(hf, y, dest0, dest1, gates, fw, final_norm)
    return out.reshape(b, s, d)


def kernel(x, conv_norm_w, conv_pw1_w, conv_pw1_b, conv_dw_w, conv_dw_b, conv_ln_g, conv_ln_b,
           conv_pw2_w, conv_pw2_b, hgrn_norm_w, hgrn_w_in, hgrn_gnorm_w, hgrn_w_out, lower_bounds,
           ffn_norm_w, router_grp_w, router_grp_b, router_exp_w, router_exp_b,
           moe_w_gate, moe_w_up, moe_w_down, final_norm_w):
    depth = lower_bounds.shape[0]
    lb_p = jax.nn.softmax(lower_bounds.astype(F32), axis=0)
    lb_all = jnp.cumsum(lb_p, axis=0) - lb_p[0]
    h = x
    for layer in range(depth):
        j = layer // 2
        if layer % 2 == 0:
            h = _conformer_layer(h, conv_norm_w[j], conv_pw1_w[j], conv_pw1_b[j], conv_dw_w[j],
                                 conv_dw_b[j], conv_ln_g[j], conv_ln_b[j], conv_pw2_w[j],
                                 conv_pw2_b[j])
        else:
            h = _hgrn_layer(h, hgrn_norm_w[j], hgrn_w_in[j], hgrn_gnorm_w[j], hgrn_w_out[j],
                            lb_all[layer])
        h = _moe_layer(h, ffn_norm_w[layer], router_grp_w[layer], router_grp_b[layer],
                       router_exp_w[layer], router_exp_b[layer], moe_w_gate, moe_w_up,
                       moe_w_down, final_norm_w, layer=layer, final_norm=(layer == depth - 1))
    return h
```

```python
import functools

import jax
import jax.numpy as jnp
from jax import lax
from jax.experimental import pallas as pl
from jax.experimental.pallas import tpu as pltpu

F32 = jnp.float32
BF16 = jnp.bfloat16

NORM_EPS = 1e-6
CONV_WIDTH = 31
CONV_HALO = 32
HGRN_HEADS = 8
HEAD_DIM = 128
CHUNK = 64
N_GROUPS = 4
EXPERTS_PER_GROUP = 8
N_EXPERTS = N_GROUPS * EXPERTS_PER_GROUP
LANES = 128
SUBLANES = 8
GROUP_LANE0 = N_EXPERTS

CONV_ROWS = 512
CONV_RC = 128
HGRN_ROWS = 1024
ROUTER_ROWS = 1024
MOE_BLOCK = 512
DMA_UNROLL = 8
DISPATCH_ROWS = 512
COMBINE_ROWS = 512
VMEM_LIMIT = 48 * 1024 * 1024


def _rms(x, w):
    ms = jnp.mean(x * x, axis=-1, keepdims=True)
    return x * lax.rsqrt(ms + NORM_EPS) * w


def _sigmoid(x):
    return 1.0 / (1.0 + jnp.exp(-x))


def _conv_kernel(x_ref, nw_ref, pw1_ref, b1_ref, dw_ref, dwb_ref, lng_ref, lnb_ref,
                 pw2_ref, b2_ref, o_ref, ubuf, cbuf):
    s = pl.program_id(1)
    ts = x_ref.shape[1]
    d = x_ref.shape[2]
    x = x_ref[0]
    hn = _rms(x, nw_ref[...]).astype(BF16)
    a = jnp.dot(hn, pw1_ref[...], preferred_element_type=F32) + b1_ref[...]
    u = a[:, :d] * _sigmoid(a[:, d:])

    @pl.when(s == 0)
    def _():
        ubuf[0:CONV_HALO, :] = jnp.zeros((CONV_HALO, d), F32)

    @pl.when(s > 0)
    def _():
        ubuf[0:CONV_HALO, :] = ubuf[ts:ts + CONV_HALO, :]

    ubuf[CONV_HALO:, :] = u

    off0 = CONV_HALO - (CONV_WIDTH - 1)
    win = CONV_RC + CONV_HALO
    for rc in range(ts // CONV_RC):
        for lc in range(d // LANES):
            ls = slice(lc * LANES, (lc + 1) * LANES)
            xw = ubuf[rc * CONV_RC:rc * CONV_RC + win, ls]
            acc = jnp.broadcast_to(dwb_ref[:, ls], (CONV_RC, LANES))
            for p in range(SUBLANES):
                xp = xw if p == 0 else pltpu.roll(xw, win - p, axis=0)
                for j in range(CONV_WIDTH):
                    if (off0 + j) % SUBLANES == p:
                        a0 = off0 + j - p
                        acc = acc + dw_ref[j:j + 1, ls] * xp[a0:a0 + CONV_RC, :]
            cbuf[rc * CONV_RC:(rc + 1) * CONV_RC, ls] = acc

    c = cbuf[...]
    mu = jnp.mean(c, axis=-1, keepdims=True)
    cc = c - mu
    var = jnp.mean(cc * cc, axis=-1, keepdims=True)
    n = cc * lax.rsqrt(var + NORM_EPS) * lng_ref[...] + lnb_ref[...]
    sw = (n * _sigmoid(n)).astype(BF16)
    y = jnp.dot(sw, pw2_ref[...], preferred_element_type=F32) + b2_ref[...]
    o_ref[0] = x + y


def _conformer_layer(h, nw, pw1, b1, dw, dwb, lng, lnb, pw2, b2):
    b, s, d = h.shape
    ts = CONV_ROWS
    row = lambda v: v.reshape(1, -1)
    full = lambda shape: pl.BlockSpec(shape, lambda bi, si: (0,) * len(shape))
    return pl.pallas_call(
        _conv_kernel,
        grid=(b, s // ts),
        in_specs=[
            pl.BlockSpec((1, ts, d), lambda bi, si: (bi, si, 0)),
            full((1, d)), full((d, 2 * d)), full((1, 2 * d)),
            full((CONV_WIDTH, d)), full((1, d)), full((1, d)), full((1, d)),
            full((d, d)), full((1, d)),
        ],
        out_specs=pl.BlockSpec((1, ts, d), lambda bi, si: (bi, si, 0)),
        out_shape=jax.ShapeDtypeStruct((b, s, d), F32),
        scratch_shapes=[pltpu.VMEM((ts + CONV_HALO, d), F32), pltpu.VMEM((ts, d), F32)],
        compiler_params=pltpu.CompilerParams(
            dimension_semantics=("arbitrary", "arbitrary"), vmem_limit_bytes=VMEM_LIMIT),
        name="conformer_conv",
    )(h, row(nw), pw1.astype(BF16), row(b1), dw, row(dwb), row(lng), row(lnb),
      pw2.astype(BF16), row(b2))


HGRN_GROUP = 4
GROUP_ROWS = HGRN_GROUP * CHUNK


def _hgrn_state_updates(gi, k_s, v_s, g_s, upd_s):
    rows = pl.ds(pl.multiple_of(gi * GROUP_ROWS, GROUP_ROWS), GROUP_ROWS)
    gc = g_s[rows, :]
    k = k_s[rows, :]
    chunk_of_row = lax.broadcasted_iota(jnp.int32, (GROUP_ROWS, HEAD_DIM), 0) // CHUNK
    glast = jnp.concatenate(
        [jnp.broadcast_to(gc[(j + 1) * CHUNK - 1:(j + 1) * CHUNK, :], (CHUNK, HEAD_DIM))
         for j in range(HGRN_GROUP)], axis=0)
    kd = k * jnp.exp(glast - gc)
    rhs = jnp.concatenate([jnp.where(chunk_of_row == j, kd, 0.0).astype(BF16)
                           for j in range(HGRN_GROUP)], axis=1)
    vb = v_s[rows, :].astype(BF16)
    upd = lax.dot_general(vb, rhs, (((0,), (0,)), ((), ())), preferred_element_type=F32)
    for j in range(HGRN_GROUP):
        upd_s[gi * HGRN_GROUP + j] = upd[:, j * HEAD_DIM:(j + 1) * HEAD_DIM]


def _hgrn_group(gi, q_s, k_s, v_s, g_s, o_s, sb_s):
    n = GROUP_ROWS
    rows = pl.ds(pl.multiple_of(gi * n, n), n)
    q = q_s[rows, :]
    k = k_s[rows, :]
    gc = g_s[rows, :]
    vb = v_s[rows, :].astype(BF16)
    nt = (((1,), (1,)), ((), ()))

    qg = (q * jnp.exp(gc)).astype(BF16)
    sts = sb_s[pl.ds(pl.multiple_of(gi * HGRN_GROUP * HEAD_DIM, HGRN_GROUP * HEAD_DIM),
                     HGRN_GROUP * HEAD_DIM), :]
    wide = lax.dot_general(qg, sts, nt, preferred_element_type=F32)
    inter = jnp.concatenate(
        [wide[j * CHUNK:(j + 1) * CHUNK, j * HEAD_DIM:(j + 1) * HEAD_DIM]
         for j in range(HGRN_GROUP)], axis=0)

    row = lax.broadcasted_iota(jnp.int32, (n, HEAD_DIM), 0)
    ti = lax.broadcasted_iota(jnp.int32, (n, n), 0)
    si = lax.broadcasted_iota(jnp.int32, (n, n), 1)
    tx = ti ^ si
    scores = jnp.broadcast_to(jnp.sum(q * k, axis=-1, keepdims=True), (n, n))
    end = gc
    b = 1
    while b < CHUNK:
        upper = (row & b) != 0
        e = jnp.exp(jnp.where(upper, gc - pltpu.roll(end, b, axis=0), end - gc))
        m = (jnp.where(upper, q, k) * e).astype(BF16)
        sb = lax.dot_general(m, m, nt, preferred_element_type=F32)
        scores = jnp.where(tx >= b, sb, scores)
        end = jnp.where(upper, end, pltpu.roll(end, n - b, axis=0))
        b *= 2
    scores = jnp.where((ti >= si) & (tx < CHUNK), scores, 0.0)

    intra = jnp.dot(scores.astype(BF16), vb, preferred_element_type=F32)
    o_s[rows, :] = inter + intra


def _hgrn_kernel(x_ref, nw_ref, win_ref, lb_ref, gn_ref, wout_ref, o_ref,
                 hn_s, proj_s, q_s, k_s, v_s, g_s, z_s, o_s, oh_s, upd_s, sb_s, st_ref):
    s = pl.program_id(1)
    h = pl.program_id(2)
    nh = pl.num_programs(2)
    ts = x_ref.shape[1]

    @pl.when(h == 0)
    def _():
        hn_s[...] = _rms(x_ref[0], nw_ref[...]).astype(BF16)
        proj_s[...] = jnp.dot(hn_s[...], win_ref[0], preferred_element_type=F32)

    @pl.when(s == 0)
    def _():
        st_ref[h] = jnp.zeros((HEAD_DIM, HEAD_DIM), F32)

    proj = proj_s[...]
    qr = proj[:, 0:HEAD_DIM]
    q_s[...] = qr * _sigmoid(qr)
    lb = lb_ref[0]
    forget = lb + (1.0 - lb) * _sigmoid(proj[:, HEAD_DIM:2 * HEAD_DIM])
    k_s[...] = 1.0 - forget
    v_s[...] = proj[:, 2 * HEAD_DIM:3 * HEAD_DIM]
    zr = proj[:, 3 * HEAD_DIM:4 * HEAD_DIM]
    z_s[...] = zr * _sigmoid(zr)

    g = jnp.log(forget)
    rmod = lax.broadcasted_iota(jnp.int32, (ts, 1), 0) % CHUNK
    dstep = 1
    while dstep < CHUNK:
        g = g + jnp.where(rmod >= dstep, pltpu.roll(g, dstep, axis=0), 0.0)
        dstep *= 2
    g_s[...] = g

    proj_s[...] = jnp.dot(hn_s[...], win_ref[jnp.minimum(h + 1, nh - 1)],
                          preferred_element_type=F32)

    nchunk = ts // CHUNK
    ngroup = ts // GROUP_ROWS
    for gi in range(ngroup):
        _hgrn_state_updates(gi, k_s, v_s, g_s, upd_s)

    st = st_ref[h]
    for c in range(nchunk):
        sb_s[c * HEAD_DIM:(c + 1) * HEAD_DIM, :] = st.astype(BF16)
        glast = g_s[(c + 1) * CHUNK - 1:(c + 1) * CHUNK, :]
        st = jnp.exp(glast) * st + upd_s[c]
    st_ref[h] = st

    for gi in range(ngroup):
        _hgrn_group(gi, q_s, k_s, v_s, g_s, o_s, sb_s)

    o = o_s[...]
    o = o * lax.rsqrt(jnp.mean(o * o, axis=-1, keepdims=True) + NORM_EPS) * gn_ref[...]
    oh_s[h] = (o * z_s[...]).astype(BF16)

    @pl.when(h == pl.num_programs(2) - 1)
    def _():
        oall = jnp.concatenate([oh_s[hh] for hh in range(HGRN_HEADS)], axis=1)
        o_ref[0] = x_ref[0] + jnp.dot(oall, wout_ref[...], preferred_element_type=F32)


def _hgrn_layer(h, nw, w_in, gn, w_out, lb):
    b, s, d = h.shape
    ts = HGRN_ROWS
    nh = HGRN_HEADS
    win_r = w_in.astype(BF16).reshape(d, 4, nh, HEAD_DIM).transpose(2, 0, 1, 3).reshape(
        nh, d, 4 * HEAD_DIM)
    return pl.pallas_call(
        _hgrn_kernel,
        grid=(b, s // ts, nh),
        in_specs=[
            pl.BlockSpec((1, ts, d), lambda bi, si, hi: (bi, si, 0)),
            pl.BlockSpec((1, d), lambda bi, si, hi: (0, 0)),
            pl.BlockSpec((nh, d, 4 * HEAD_DIM), lambda bi, si, hi: (0, 0, 0),
                         pipeline_mode=pl.Buffered(1)),
            pl.BlockSpec((1, 1, HEAD_DIM), lambda bi, si, hi: (hi, 0, 0)),
            pl.BlockSpec((1, HEAD_DIM), lambda bi, si, hi: (0, 0)),
            pl.BlockSpec((d, d), lambda bi, si, hi: (0, 0), pipeline_mode=pl.Buffered(1)),
        ],
        out_specs=pl.BlockSpec((1, ts, d), lambda bi, si, hi: (bi, si, 0)),
        out_shape=jax.ShapeDtypeStruct((b, s, d), F32),
        scratch_shapes=[
            pltpu.VMEM((ts, d), BF16),
            pltpu.VMEM((ts, 4 * HEAD_DIM), F32),
            pltpu.VMEM((ts, HEAD_DIM), F32), pltpu.VMEM((ts, HEAD_DIM), F32),
            pltpu.VMEM((ts, HEAD_DIM), F32), pltpu.VMEM((ts, HEAD_DIM), F32),
            pltpu.VMEM((ts, HEAD_DIM), F32), pltpu.VMEM((ts, HEAD_DIM), F32),
            pltpu.VMEM((nh, ts, HEAD_DIM), BF16),
            pltpu.VMEM((ts // CHUNK, HEAD_DIM, HEAD_DIM), F32),
            pltpu.VMEM((ts // CHUNK * HEAD_DIM, HEAD_DIM), BF16),
            pltpu.VMEM((nh, HEAD_DIM, HEAD_DIM), F32),
        ],
        compiler_params=pltpu.CompilerParams(
            dimension_semantics=("arbitrary", "arbitrary", "arbitrary"),
            vmem_limit_bytes=VMEM_LIMIT),
        name="hgrn2",
    )(h, nw.reshape(1, d), win_r, lb.reshape(nh, 1, HEAD_DIM),
      gn.reshape(1, HEAD_DIM), w_out.astype(BF16))


def _router_kernel(h_ref, nw_ref, wr_ref, br_ref, idx_ref, gate_ref, cnt_ref, cnt_s):
    i = pl.program_id(0)
    tm = h_ref.shape[0]

    @pl.when(i == 0)
    def _():
        cnt_s[...] = jnp.zeros_like(cnt_s)

    hn = _rms(h_ref[...], nw_ref[...])
    h1 = hn.astype(BF16)
    h2 = (hn - h1.astype(F32)).astype(BF16)
    w = wr_ref[...]
    w1 = w.astype(BF16)
    w2 = (w - w1.astype(F32)).astype(BF16)
    logits = (jnp.dot(h1, w1, preferred_element_type=F32) + jnp.dot(h1, w2, preferred_element_type=F32)
              + jnp.dot(h2, w1, preferred_element_type=F32)) + br_ref[...]
    li = lax.broadcasted_iota(jnp.int32, (tm, LANES), 1).astype(F32)
    neg = jnp.float32(-jnp.inf)
    big = jnp.float32(LANES)

    gl = jnp.where((li >= GROUP_LANE0) & (li < GROUP_LANE0 + N_GROUPS), logits, neg)
    gmax = jnp.max(gl, axis=-1, keepdims=True)
    gval = 1.0 / jnp.sum(jnp.exp(gl - gmax), axis=-1, keepdims=True)
    gidx = jnp.min(jnp.where(gl == gmax, li, big), axis=-1, keepdims=True) - GROUP_LANE0

    lo = gidx * EXPERTS_PER_GROUP
    el = jnp.where((li >= lo) & (li < lo + EXPERTS_PER_GROUP), logits, neg)
    v0 = jnp.max(el, axis=-1, keepdims=True)
    i0 = jnp.min(jnp.where(el == v0, li, big), axis=-1, keepdims=True)
    el2 = jnp.where(li == i0, neg, el)
    v1 = jnp.max(el2, axis=-1, keepdims=True)
    i1 = jnp.min(jnp.where(el2 == v1, li, big), axis=-1, keepdims=True)
    t = jnp.exp(v1 - v0)
    p0 = 1.0 / (1.0 + t)
    g0 = gval * p0
    g1 = gval * (t * p0)

    oh = jnp.where((li == i0) | (li == i1), 1.0, 0.0)
    rr = lax.broadcasted_iota(jnp.int32, (tm, tm), 0)
    cc = lax.broadcasted_iota(jnp.int32, (tm, tm), 1)
    lower = jnp.where(rr > cc, 1.0, 0.0).astype(BF16)
    before = jnp.dot(lower, oh.astype(BF16), preferred_element_type=F32) + cnt_s[...]
    r0 = jnp.sum(jnp.where(li == i0, before, 0.0), axis=-1, keepdims=True)
    r1 = jnp.sum(jnp.where(li == i1, before, 0.0), axis=-1, keepdims=True)
    cnt_s[...] = cnt_s[...] + jnp.sum(oh, axis=0, keepdims=True)

    rec = jnp.where(li == 0.0, i0, jnp.where(li == 1.0, i1,
                    jnp.where(li == 2.0, r0, jnp.where(li == 3.0, r1, 0.0))))
    idx_ref[...] = rec.T[0:8, :]
    gate_ref[...] = jnp.where(li == 0.0, g0, jnp.where(li == 1.0, g1, 0.0))
    cnt_ref[...] = cnt_s[...]


def _router(hf, nw, grp_w, grp_b, exp_w, exp_b):
    t, d = hf.shape
    tm = ROUTER_ROWS
    wr = jnp.zeros((d, LANES), F32)
    wr = wr.at[:, :N_EXPERTS].set(exp_w.reshape(d, N_EXPERTS))
    wr = wr.at[:, GROUP_LANE0:GROUP_LANE0 + N_GROUPS].set(grp_w)
    br = jnp.zeros((1, LANES), F32)
    br = br.at[0, :N_EXPERTS].set(exp_b.reshape(N_EXPERTS))
    br = br.at[0, GROUP_LANE0:GROUP_LANE0 + N_GROUPS].set(grp_b)
    return pl.pallas_call(
        _router_kernel,
        grid=(t // tm,),
        in_specs=[
            pl.BlockSpec((tm, d), lambda i: (i, 0)),
            pl.BlockSpec((1, d), lambda i: (0, 0)),
            pl.BlockSpec((d, LANES), lambda i: (0, 0)),
            pl.BlockSpec((1, LANES), lambda i: (0, 0)),
        ],
        out_specs=[
            pl.BlockSpec((8, tm), lambda i: (0, i)),
            pl.BlockSpec((tm, LANES), lambda i: (i, 0)),
            pl.BlockSpec((1, LANES), lambda i: (0, 0)),
        ],
        out_shape=[
            jax.ShapeDtypeStruct((8, t), F32),
            jax.ShapeDtypeStruct((t, LANES), F32),
            jax.ShapeDtypeStruct((1, LANES), F32),
        ],
        scratch_shapes=[pltpu.VMEM((1, LANES), F32)],
        compiler_params=pltpu.CompilerParams(
            dimension_semantics=("arbitrary",), vmem_limit_bytes=VMEM_LIMIT),
        name="moe_router",
    )(hf, nw.reshape(1, d), wr, br)


def _store_row_tiles(ref, x):
    n = x.shape[0]
    for j in range(SUBLANES):
        ref[pl.ds(j, n, stride=SUBLANES), :] = x[:, j * LANES:(j + 1) * LANES]


def _load_row_tiles(ref):
    n = ref.shape[0] // SUBLANES
    return jnp.concatenate([ref[pl.ds(j, n, stride=SUBLANES), :] for j in range(SUBLANES)], axis=1)


def _row_tile(ref, r):
    return ref.at[pl.ds(pl.multiple_of(r * SUBLANES, SUBLANES), SUBLANES)]


def _dispatch_kernel(d0_ref, d1_ref, zf_ref, h_ref, nw_ref, xs_hbm, stage, zbuf, sem, zsem):
    i = pl.program_id(0)
    nb = pl.num_programs(0)
    tm = h_ref.shape[0]
    blk = zbuf.shape[0]
    slot = i % 2

    def wait_slot(s):
        for _ in range(2):
            pltpu.make_async_copy(xs_hbm.at[pl.ds(0, tm)], xs_hbm.at[pl.ds(0, tm)], sem.at[s]).wait()

    @pl.when(i == 0)
    def _():
        zbuf[...] = jnp.zeros_like(zbuf)

        def zero_block(start):
            return pltpu.make_async_copy(zbuf, xs_hbm.at[pl.ds(start, blk)], zsem)

        def zbody(e, started):
            @pl.when(zf_ref[e] >= 0)
            def _():
                zero_block(zf_ref[e]).start()
            return started + jnp.where(zf_ref[e] >= 0, 1, 0)
        started = lax.fori_loop(0, N_EXPERTS, zbody, 0)

        def tbody(b, carry):
            zero_block(b * blk).start()
            return carry
        n_blocks = xs_hbm.shape[0] // blk
        lax.fori_loop(zf_ref[N_EXPERTS], n_blocks, tbody, 0)

        def wbody(_, carry):
            zero_block(0).wait()
            return carry
        lax.fori_loop(0, started + n_blocks - zf_ref[N_EXPERTS], wbody, 0)

    @pl.when(i >= 2)
    def _():
        wait_slot(slot)

    _store_row_tiles(stage.at[slot], _rms(h_ref[...], nw_ref[...]))

    def body(rr, carry):
        for j in range(DMA_UNROLL):
            r = rr * DMA_UNROLL + j
            for kk, d_ref in enumerate((d0_ref, d1_ref)):
                pltpu.make_async_copy(_row_tile(stage.at[slot], r), xs_hbm.at[d_ref[i * tm + r]],
                                      sem.at[slot]).start(priority=kk)
        return carry
    lax.fori_loop(0, tm // DMA_UNROLL, body, 0)

    @pl.when(i == nb - 1)
    def _():
        wait_slot(slot)

        @pl.when(nb >= 2)
        def _():
            wait_slot(1 - slot)


def _dispatch(hf, nw, dest0, dest1, zfill, n_rows):
    t, d = hf.shape
    tm = DISPATCH_ROWS
    grid_spec = pltpu.PrefetchScalarGridSpec(
        num_scalar_prefetch=3,
        grid=(t // tm,),
        in_specs=[
            pl.BlockSpec((tm, d), lambda i, a, b, c: (i, 0)),
            pl.BlockSpec((1, d), lambda i, a, b, c: (0, 0)),
        ],
        out_specs=pl.BlockSpec(memory_space=pl.ANY),
        scratch_shapes=[pltpu.VMEM((2, tm * SUBLANES, d // SUBLANES), F32),
                        pltpu.VMEM((MOE_BLOCK, SUBLANES, d // SUBLANES), F32),
                        pltpu.SemaphoreType.DMA((2,)), pltpu.SemaphoreType.DMA(())],
    )
    return pl.pallas_call(
        _dispatch_kernel,
        grid_spec=grid_spec,
        out_shape=jax.ShapeDtypeStruct((n_rows, SUBLANES, d // SUBLANES), F32),
        compiler_params=pltpu.CompilerParams(
            dimension_semantics=("arbitrary",), vmem_limit_bytes=VMEM_LIMIT),
        name="moe_dispatch",
    )(dest0, dest1, zfill, hf, nw.reshape(1, d))


def _expert_kernel(be_ref, nu_ref, x_ref, wg_ref, wu_ref, wd_ref, y_ref, wg_s, wu_s, wd_s):
    i = pl.program_id(0)
    prev = be_ref[jnp.maximum(i - 1, 0)]

    @pl.when(i < nu_ref[0])
    def _():
        @pl.when((i == 0) | (be_ref[i] != prev))
        def _():
            wg_s[...] = wg_ref[0, 0].astype(BF16)
            wu_s[...] = wu_ref[0, 0].astype(BF16)
            wd_s[...] = wd_ref[0, 0].astype(BF16)

        hn = _load_row_tiles(x_ref).astype(BF16)
        g = jnp.dot(hn, wg_s[...], preferred_element_type=F32)
        u = jnp.dot(hn, wu_s[...], preferred_element_type=F32)
        a = (g * _sigmoid(g) * u).astype(BF16)
        _store_row_tiles(y_ref, jnp.dot(a, wd_s[...], preferred_element_type=F32))

    @pl.when(i >= nu_ref[0])
    def _():
        y_ref[...] = jnp.zeros_like(y_ref)


def _experts(xs, block_expert, n_used, w_gate, w_up, w_down, layer):
    n_rows = xs.shape[0]
    blk = MOE_BLOCK
    d, de = w_gate.shape[-2:]
    row_tile = (blk * SUBLANES, d // SUBLANES)
    used_blk = lambda i, be, nu: (jnp.minimum(i, nu[0] - 1), 0)
    grid_spec = pltpu.PrefetchScalarGridSpec(
        num_scalar_prefetch=2,
        grid=(n_rows // blk,),
        in_specs=[
            pl.BlockSpec(row_tile, used_blk),
            pl.BlockSpec((1, 1, d, de), lambda i, be, nu: (layer, be[i], 0, 0)),
            pl.BlockSpec((1, 1, d, de), lambda i, be, nu: (layer, be[i], 0, 0)),
            pl.BlockSpec((1, 1, de, d), lambda i, be, nu: (layer, be[i], 0, 0)),
        ],
        out_specs=pl.BlockSpec(row_tile, lambda i, be, nu: (i, 0)),
        scratch_shapes=[pltpu.VMEM((d, de), BF16), pltpu.VMEM((d, de), BF16),
                        pltpu.VMEM((de, d), BF16)],
    )
    return pl.pallas_call(
        _expert_kernel,
        grid_spec=grid_spec,
        out_shape=jax.ShapeDtypeStruct((n_rows * SUBLANES, d // SUBLANES), F32),
        compiler_params=pltpu.CompilerParams(
            dimension_semantics=("arbitrary",), vmem_limit_bytes=VMEM_LIMIT),
        name="moe_experts",
    )(block_expert, n_used, xs.reshape(n_rows * SUBLANES, d // SUBLANES), w_gate, w_up, w_down)


def _combine_kernel(d0_ref, d1_ref, y_hbm, h_ref, gate_ref, fw_ref, o_ref, ybuf, sem, *, final_norm):
    i = pl.program_id(0)
    nb = pl.num_programs(0)
    tc = h_ref.shape[0]

    def issue(b, slot):
        def body(rr, carry):
            for j in range(DMA_UNROLL):
                r = rr * DMA_UNROLL + j
                for kk, d_ref in enumerate((d0_ref, d1_ref)):
                    pltpu.make_async_copy(y_hbm.at[d_ref[b * tc + r]], _row_tile(ybuf.at[slot, kk], r),
                                          sem.at[slot]).start(priority=kk)
            return carry
        lax.fori_loop(0, tc // DMA_UNROLL, body, 0)

    @pl.when(i == 0)
    def _():
        issue(0, 0)

    @pl.when(i + 1 < nb)
    def _():
        issue(i + 1, (i + 1) % 2)

    slot = i % 2
    for kk in range(2):
        pltpu.make_async_copy(y_hbm.at[pl.ds(0, tc)], y_hbm.at[pl.ds(0, tc)], sem.at[slot]).wait()

    gate = gate_ref[...]
    y0 = _load_row_tiles(ybuf.at[slot, 0])
    y1 = _load_row_tiles(ybuf.at[slot, 1])
    o = h_ref[...] + (gate[:, 0:1] * y0 + gate[:, 1:2] * y1)
    if final_norm:
        o = _rms(o, fw_ref[...])
    o_ref[...] = o


def _combine(hf, y, dest0, dest1, gates, fw, final_norm):
    t, d = hf.shape
    tc = COMBINE_ROWS
    grid_spec = pltpu.PrefetchScalarGridSpec(
        num_scalar_prefetch=2,
        grid=(t // tc,),
        in_specs=[
            pl.BlockSpec(memory_space=pl.ANY),
            pl.BlockSpec((tc, d), lambda i, a, b: (i, 0)),
            pl.BlockSpec((tc, LANES), lambda i, a, b: (i, 0)),
            pl.BlockSpec((1, d), lambda i, a, b: (0, 0)),
        ],
        out_specs=pl.BlockSpec((tc, d), lambda i, a, b: (i, 0)),
        scratch_shapes=[pltpu.VMEM((2, 2, tc * SUBLANES, d // SUBLANES), F32),
                        pltpu.SemaphoreType.DMA((2,))],
    )
    return pl.pallas_call(
        functools.partial(_combine_kernel, final_norm=final_norm),
        grid_spec=grid_spec,
        out_shape=jax.ShapeDtypeStruct((t, d), F32),
        compiler_params=pltpu.CompilerParams(
            dimension_semantics=("arbitrary",), vmem_limit_bytes=VMEM_LIMIT),
        name="moe_combine",
    )(dest0, dest1, y.reshape(-1, SUBLANES, d // SUBLANES), hf, gates, fw.reshape(1, d))


def _moe_layer(h, nw, grp_w, grp_b, exp_w, exp_b, w_gate, w_up, w_down, fw, layer, final_norm):
    b, s, d = h.shape
    t = b * s
    hf = h.reshape(t, d)
    rec, gates, cnt = _router(hf, nw, grp_w, grp_b, exp_w, exp_b)

    rec = rec.astype(jnp.int32)
    counts = cnt[0, :N_EXPERTS].astype(jnp.int32)
    padded = (counts + MOE_BLOCK - 1) // MOE_BLOCK * MOE_BLOCK
    pad_end = jnp.cumsum(padded)
    pad_start = pad_end - padded
    eids = jnp.arange(N_EXPERTS, dtype=jnp.int32)[:, None]
    dest0 = jnp.sum(jnp.where(rec[0][None, :] == eids, pad_start[:, None], 0), axis=0) + rec[2]
    dest1 = jnp.sum(jnp.where(rec[1][None, :] == eids, pad_start[:, None], 0), axis=0) + rec[3]
    n_rows = 2 * t + N_EXPERTS * MOE_BLOCK
    n_blocks = n_rows // MOE_BLOCK
    block_start = jnp.arange(n_blocks, dtype=jnp.int32) * MOE_BLOCK
    block_expert = jnp.minimum(
        jnp.sum((block_start[:, None] >= pad_end[None, :]).astype(jnp.int32), axis=1), N_EXPERTS - 1)
    n_used = (pad_end[-1:] // MOE_BLOCK).astype(jnp.int32)
    zfill = jnp.concatenate([jnp.where(padded > 0, pad_end - MOE_BLOCK, -1).astype(jnp.int32), n_used])

    xs = _dispatch(hf, nw, dest0, dest1, zfill, n_rows)
    y = _experts(xs, block_expert, n_used, w_gate, w_up, w_down, layer)
    out = _combine(hf, y, dest0, dest1, gates, fw, final_norm)
    return out.reshape(b, s, d)


def kernel(x, conv_norm_w, conv_pw1_w, conv_pw1_b, conv_dw_w, conv_dw_b, conv_ln_g, conv_ln_b,
           conv_pw2_w, conv_pw2_b, hgrn_norm_w, hgrn_w_in, hgrn_gnorm_w, hgrn_w_out, lower_bounds,
           ffn_norm_w, router_grp_w, router_grp_b, router_exp_w, router_exp_b,
           moe_w_gate, moe_w_up, moe_w_down, final_norm_w):
    depth = lower_bounds.shape[0]
    lb_p = jax.nn.softmax(lower_bounds.astype(F32), axis=0)
    lb_all = jnp.cumsum(lb_p, axis=0) - lb_p[0]
    h = x
    for layer in range(depth):
        j = layer // 2
        if layer % 2 == 0:
            h = _conformer_layer(h, conv_norm_w[j], conv_pw1_w[j], conv_pw1_b[j], conv_dw_w[j],
                                 conv_dw_b[j], conv_ln_g[j], conv_ln_b[j], conv_pw2_w[j],
                                 conv_pw2_b[j])
        else:
            h = _hgrn_layer(h, hgrn_norm_w[j], hgrn_w_in[j], hgrn_gnorm_w[j], hgrn_w_out[j],
                            lb_all[layer])
        h = _moe_layer(h, ffn_norm_w[layer], router_grp_w[layer], router_grp_b[layer],
                       router_exp_w[layer], router_exp_b[layer], moe_w_gate, moe_w_up,
                       moe_w_down, final_norm_w, layer=layer, final_norm=(layer == depth - 1))
    return h
```

```python
import functools

import jax
import jax.numpy as jnp
from jax import lax
from jax.experimental import pallas as pl
from jax.experimental.pallas import tpu as pltpu

F32 = jnp.float32
BF16 = jnp.bfloat16

NORM_EPS = 1e-6
CONV_WIDTH = 31
CONV_HALO = 32
HGRN_HEADS = 8
HEAD_DIM = 128
CHUNK = 64
N_GROUPS = 4
EXPERTS_PER_GROUP = 8
N_EXPERTS = N_GROUPS * EXPERTS_PER_GROUP
LANES = 128
SUBLANES = 8
GROUP_LANE0 = N_EXPERTS

CONV_ROWS = 512
CONV_RC = 128
HGRN_ROWS = 1024
ROUTER_ROWS = 1024
MOE_BLOCK = 512
DMA_UNROLL = 8
DISPATCH_ROWS = 1024
COMBINE_ROWS = 1024
VMEM_LIMIT = 48 * 1024 * 1024


def _rms(x, w):
    ms = jnp.mean(x * x, axis=-1, keepdims=True)
    return x * lax.rsqrt(ms + NORM_EPS) * w


def _sigmoid(x):
    return 1.0 / (1.0 + jnp.exp(-x))


def _conv_kernel(x_ref, nw_ref, pw1_ref, b1_ref, dw_ref, dwb_ref, lng_ref, lnb_ref,
                 pw2_ref, b2_ref, o_ref, ubuf, cbuf):
    s = pl.program_id(1)
    ts = x_ref.shape[1]
    d = x_ref.shape[2]
    x = x_ref[0]
    hn = _rms(x, nw_ref[...]).astype(BF16)
    a = jnp.dot(hn, pw1_ref[...], preferred_element_type=F32) + b1_ref[...]
    u = a[:, :d] * _sigmoid(a[:, d:])

    @pl.when(s == 0)
    def _():
        ubuf[0:CONV_HALO, :] = jnp.zeros((CONV_HALO, d), F32)

    @pl.when(s > 0)
    def _():
        ubuf[0:CONV_HALO, :] = ubuf[ts:ts + CONV_HALO, :]

    ubuf[CONV_HALO:, :] = u

    off0 = CONV_HALO - (CONV_WIDTH - 1)
    win = CONV_RC + CONV_HALO
    for rc in range(ts // CONV_RC):
        for lc in range(d // LANES):
            ls = slice(lc * LANES, (lc + 1) * LANES)
            xw = ubuf[rc * CONV_RC:rc * CONV_RC + win, ls]
            acc = jnp.broadcast_to(dwb_ref[:, ls], (CONV_RC, LANES))
            for p in range(SUBLANES):
                xp = xw if p == 0 else pltpu.roll(xw, win - p, axis=0)
                for j in range(CONV_WIDTH):
                    if (off0 + j) % SUBLANES == p:
                        a0 = off0 + j - p
                        acc = acc + dw_ref[j:j + 1, ls] * xp[a0:a0 + CONV_RC, :]
            cbuf[rc * CONV_RC:(rc + 1) * CONV_RC, ls] = acc

    c = cbuf[...]
    mu = jnp.mean(c, axis=-1, keepdims=True)
    cc = c - mu
    var = jnp.mean(cc * cc, axis=-1, keepdims=True)
    n = cc * lax.rsqrt(var + NORM_EPS) * lng_ref[...] + lnb_ref[...]
    sw = (n * _sigmoid(n)).astype(BF16)
    y = jnp.dot(sw, pw2_ref[...], preferred_element_type=F32) + b2_ref[...]
    o_ref[0] = x + y


def _conformer_layer(h, nw, pw1, b1, dw, dwb, lng, lnb, pw2, b2):
    b, s, d = h.shape
    ts = CONV_ROWS
    row = lambda v: v.reshape(1, -1)
    full = lambda shape: pl.BlockSpec(shape, lambda bi, si: (0,) * len(shape))
    return pl.pallas_call(
        _conv_kernel,
        grid=(b, s // ts),
        in_specs=[
            pl.BlockSpec((1, ts, d), lambda bi, si: (bi, si, 0)),
            full((1, d)), full((d, 2 * d)), full((1, 2 * d)),
            full((CONV_WIDTH, d)), full((1, d)), full((1, d)), full((1, d)),
            full((d, d)), full((1, d)),
        ],
        out_specs=pl.BlockSpec((1, ts, d), lambda bi, si: (bi, si, 0)),
        out_shape=jax.ShapeDtypeStruct((b, s, d), F32),
        scratch_shapes=[pltpu.VMEM((ts + CONV_HALO, d), F32), pltpu.VMEM((ts, d), F32)],
        compiler_params=pltpu.CompilerParams(
            dimension_semantics=("arbitrary", "arbitrary"), vmem_limit_bytes=VMEM_LIMIT),
        name="conformer_conv",
    )(h, row(nw), pw1.astype(BF16), row(b1), dw, row(dwb), row(lng), row(lnb),
      pw2.astype(BF16), row(b2))


HGRN_GROUP = 4
GROUP_ROWS = HGRN_GROUP * CHUNK


def _hgrn_state_updates(gi, k_s, v_s, g_s, upd_s):
    rows = pl.ds(pl.multiple_of(gi * GROUP_ROWS, GROUP_ROWS), GROUP_ROWS)
    gc = g_s[rows, :]
    k = k_s[rows, :]
    chunk_of_row = lax.broadcasted_iota(jnp.int32, (GROUP_ROWS, HEAD_DIM), 0) // CHUNK
    glast = jnp.concatenate(
        [jnp.broadcast_to(gc[(j + 1) * CHUNK - 1:(j + 1) * CHUNK, :], (CHUNK, HEAD_DIM))
         for j in range(HGRN_GROUP)], axis=0)
    kd = k * jnp.exp(glast - gc)
    rhs = jnp.concatenate([jnp.where(chunk_of_row == j, kd, 0.0).astype(BF16)
                           for j in range(HGRN_GROUP)], axis=1)
    vb = v_s[rows, :].astype(BF16)
    upd = lax.dot_general(vb, rhs, (((0,), (0,)), ((), ())), preferred_element_type=F32)
    for j in range(HGRN_GROUP):
        upd_s[gi * HGRN_GROUP + j] = upd[:, j * HEAD_DIM:(j + 1) * HEAD_DIM]


def _hgrn_group(gi, q_s, k_s, v_s, g_s, o_s, sb_s):
    n = GROUP_ROWS
    rows = pl.ds(pl.multiple_of(gi * n, n), n)
    q = q_s[rows, :]
    k = k_s[rows, :]
    gc = g_s[rows, :]
    vb = v_s[rows, :].astype(BF16)
    nt = (((1,), (1,)), ((), ()))

    qg = (q * jnp.exp(gc)).astype(BF16)
    sts = sb_s[pl.ds(pl.multiple_of(gi * HGRN_GROUP * HEAD_DIM, HGRN_GROUP * HEAD_DIM),
                     HGRN_GROUP * HEAD_DIM), :]
    wide = lax.dot_general(qg, sts, nt, preferred_element_type=F32)
    inter = jnp.concatenate(
        [wide[j * CHUNK:(j + 1) * CHUNK, j * HEAD_DIM:(j + 1) * HEAD_DIM]
         for j in range(HGRN_GROUP)], axis=0)

    row = lax.broadcasted_iota(jnp.int32, (n, HEAD_DIM), 0)
    ti = lax.broadcasted_iota(jnp.int32, (n, n), 0)
    si = lax.broadcasted_iota(jnp.int32, (n, n), 1)
    tx = ti ^ si
    scores = jnp.broadcast_to(jnp.sum(q * k, axis=-1, keepdims=True), (n, n))
    end = gc
    b = 1
    while b < CHUNK:
        upper = (row & b) != 0
        e = jnp.exp(jnp.where(upper, gc - pltpu.roll(end, b, axis=0), end - gc))
        m = (jnp.where(upper, q, k) * e).astype(BF16)
        sb = lax.dot_general(m, m, nt, preferred_element_type=F32)
        scores = jnp.where(tx >= b, sb, scores)
        end = jnp.where(upper, end, pltpu.roll(end, n - b, axis=0))
        b *= 2
    scores = jnp.where((ti >= si) & (tx < CHUNK), scores, 0.0)

    intra = jnp.dot(scores.astype(BF16), vb, preferred_element_type=F32)
    o_s[rows, :] = inter + intra


def _hgrn_kernel(x_ref, nw_ref, win_ref, lb_ref, gn_ref, wout_ref, o_ref,
                 hn_s, proj_s, q_s, k_s, v_s, g_s, z_s, o_s, oh_s, upd_s, sb_s, st_ref):
    s = pl.program_id(1)
    h = pl.program_id(2)
    nh = pl.num_programs(2)
    ts = x_ref.shape[1]

    @pl.when(h == 0)
    def _():
        hn_s[...] = _rms(x_ref[0], nw_ref[...]).astype(BF16)
        proj_s[...] = jnp.dot(hn_s[...], win_ref[0], preferred_element_type=F32)

    @pl.when(s == 0)
    def _():
        st_ref[h] = jnp.zeros((HEAD_DIM, HEAD_DIM), F32)

    proj = proj_s[...]
    qr = proj[:, 0:HEAD_DIM]
    q_s[...] = qr * _sigmoid(qr)
    lb = lb_ref[0]
    forget = lb + (1.0 - lb) * _sigmoid(proj[:, HEAD_DIM:2 * HEAD_DIM])
    k_s[...] = 1.0 - forget
    v_s[...] = proj[:, 2 * HEAD_DIM:3 * HEAD_DIM]
    zr = proj[:, 3 * HEAD_DIM:4 * HEAD_DIM]
    z_s[...] = zr * _sigmoid(zr)

    g = jnp.log(forget)
    rmod = lax.broadcasted_iota(jnp.int32, (ts, 1), 0) % CHUNK
    dstep = 1
    while dstep < CHUNK:
        g = g + jnp.where(rmod >= dstep, pltpu.roll(g, dstep, axis=0), 0.0)
        dstep *= 2
    g_s[...] = g

    proj_s[...] = jnp.dot(hn_s[...], win_ref[jnp.minimum(h + 1, nh - 1)],
                          preferred_element_type=F32)

    nchunk = ts // CHUNK
    ngroup = ts // GROUP_ROWS
    for gi in range(ngroup):
        _hgrn_state_updates(gi, k_s, v_s, g_s, upd_s)

    st = st_ref[h]
    for c in range(nchunk):
        sb_s[c * HEAD_DIM:(c + 1) * HEAD_DIM, :] = st.astype(BF16)
        glast = g_s[(c + 1) * CHUNK - 1:(c + 1) * CHUNK, :]
        st = jnp.exp(glast) * st + upd_s[c]
    st_ref[h] = st

    for gi in range(ngroup):
        _hgrn_group(gi, q_s, k_s, v_s, g_s, o_s, sb_s)

    o = o_s[...]
    o = o * lax.rsqrt(jnp.mean(o * o, axis=-1, keepdims=True) + NORM_EPS) * gn_ref[...]
    oh_s[h] = (o * z_s[...]).astype(BF16)

    @pl.when(h == pl.num_programs(2) - 1)
    def _():
        oall = jnp.concatenate([oh_s[hh] for hh in range(HGRN_HEADS)], axis=1)
        o_ref[0] = x_ref[0] + jnp.dot(oall, wout_ref[...], preferred_element_type=F32)


def _hgrn_layer(h, nw, w_in, gn, w_out, lb):
    b, s, d = h.shape
    ts = HGRN_ROWS
    nh = HGRN_HEADS
    win_r = w_in.astype(BF16).reshape(d, 4, nh, HEAD_DIM).transpose(2, 0, 1, 3).reshape(
        nh, d, 4 * HEAD_DIM)
    return pl.pallas_call(
        _hgrn_kernel,
        grid=(b, s // ts, nh),
        in_specs=[
            pl.BlockSpec((1, ts, d), lambda bi, si, hi: (bi, si, 0)),
            pl.BlockSpec((1, d), lambda bi, si, hi: (0, 0)),
            pl.BlockSpec((nh, d, 4 * HEAD_DIM), lambda bi, si, hi: (0, 0, 0),
                         pipeline_mode=pl.Buffered(1)),
            pl.BlockSpec((1, 1, HEAD_DIM), lambda bi, si, hi: (hi, 0, 0)),
            pl.BlockSpec((1, HEAD_DIM), lambda bi, si, hi: (0, 0)),
            pl.BlockSpec((d, d), lambda bi, si, hi: (0, 0), pipeline_mode=pl.Buffered(1)),
        ],
        out_specs=pl.BlockSpec((1, ts, d), lambda bi, si, hi: (bi, si, 0)),
        out_shape=jax.ShapeDtypeStruct((b, s, d), F32),
        scratch_shapes=[
            pltpu.VMEM((ts, d), BF16),
            pltpu.VMEM((ts, 4 * HEAD_DIM), F32),
            pltpu.VMEM((ts, HEAD_DIM), F32), pltpu.VMEM((ts, HEAD_DIM), F32),
            pltpu.VMEM((ts, HEAD_DIM), F32), pltpu.VMEM((ts, HEAD_DIM), F32),
            pltpu.VMEM((ts, HEAD_DIM), F32), pltpu.VMEM((ts, HEAD_DIM), F32),
            pltpu.VMEM((nh, ts, HEAD_DIM), BF16),
            pltpu.VMEM((ts // CHUNK, HEAD_DIM, HEAD_DIM), F32),
            pltpu.VMEM((ts // CHUNK * HEAD_DIM, HEAD_DIM), BF16),
            pltpu.VMEM((nh, HEAD_DIM, HEAD_DIM), F32),
        ],
        compiler_params=pltpu.CompilerParams(
            dimension_semantics=("arbitrary", "arbitrary", "arbitrary"),
            vmem_limit_bytes=VMEM_LIMIT),
        name="hgrn2",
    )(h, nw.reshape(1, d), win_r, lb.reshape(nh, 1, HEAD_DIM),
      gn.reshape(1, HEAD_DIM), w_out.astype(BF16))


def _router_kernel(h_ref, nw_ref, wr_ref, br_ref, idx_ref, gate_ref, cnt_ref, cnt_s):
    i = pl.program_id(0)
    tm = h_ref.shape[0]

    @pl.when(i == 0)
    def _():
        cnt_s[...] = jnp.zeros_like(cnt_s)

    hn = _rms(h_ref[...], nw_ref[...])
    h1 = hn.astype(BF16)
    h2 = (hn - h1.astype(F32)).astype(BF16)
    w = wr_ref[...]
    w1 = w.astype(BF16)
    w2 = (w - w1.astype(F32)).astype(BF16)
    logits = (jnp.dot(h1, w1, preferred_element_type=F32) + jnp.dot(h1, w2, preferred_element_type=F32)
              + jnp.dot(h2, w1, preferred_element_type=F32)) + br_ref[...]
    li = lax.broadcasted_iota(jnp.int32, (tm, LANES), 1).astype(F32)
    neg = jnp.float32(-jnp.inf)
    big = jnp.float32(LANES)

    gl = jnp.where((li >= GROUP_LANE0) & (li < GROUP_LANE0 + N_GROUPS), logits, neg)
    gmax = jnp.max(gl, axis=-1, keepdims=True)
    gval = 1.0 / jnp.sum(jnp.exp(gl - gmax), axis=-1, keepdims=True)
    gidx = jnp.min(jnp.where(gl == gmax, li, big), axis=-1, keepdims=True) - GROUP_LANE0

    lo = gidx * EXPERTS_PER_GROUP
    el = jnp.where((li >= lo) & (li < lo + EXPERTS_PER_GROUP), logits, neg)
    v0 = jnp.max(el, axis=-1, keepdims=True)
    i0 = jnp.min(jnp.where(el == v0, li, big), axis=-1, keepdims=True)
    el2 = jnp.where(li == i0, neg, el)
    v1 = jnp.max(el2, axis=-1, keepdims=True)
    i1 = jnp.min(jnp.where(el2 == v1, li, big), axis=-1, keepdims=True)
    t = jnp.exp(v1 - v0)
    p0 = 1.0 / (1.0 + t)
    g0 = gval * p0
    g1 = gval * (t * p0)

    oh = jnp.where((li == i0) | (li == i1), 1.0, 0.0)
    rr = lax.broadcasted_iota(jnp.int32, (tm, tm), 0)
    cc = lax.broadcasted_iota(jnp.int32, (tm, tm), 1)
    lower = jnp.where(rr > cc, 1.0, 0.0).astype(BF16)
    before = jnp.dot(lower, oh.astype(BF16), preferred_element_type=F32) + cnt_s[...]
    r0 = jnp.sum(jnp.where(li == i0, before, 0.0), axis=-1, keepdims=True)
    r1 = jnp.sum(jnp.where(li == i1, before, 0.0), axis=-1, keepdims=True)
    cnt_s[...] = cnt_s[...] + jnp.sum(oh, axis=0, keepdims=True)

    rec = jnp.where(li == 0.0, i0, jnp.where(li == 1.0, i1,
                    jnp.where(li == 2.0, r0, jnp.where(li == 3.0, r1, 0.0))))
    idx_ref[...] = rec.T[0:8, :]
    gate_ref[...] = jnp.where(li == 0.0, g0, jnp.where(li == 1.0, g1, 0.0))
    cnt_ref[...] = cnt_s[...]


def _router(hf, nw, grp_w, grp_b, exp_w, exp_b):
    t, d = hf.shape
    tm = ROUTER_ROWS
    wr = jnp.zeros((d, LANES), F32)
    wr = wr.at[:, :N_EXPERTS].set(exp_w.reshape(d, N_EXPERTS))
    wr = wr.at[:, GROUP_LANE0:GROUP_LANE0 + N_GROUPS].set(grp_w)
    br = jnp.zeros((1, LANES), F32)
    br = br.at[0, :N_EXPERTS].set(exp_b.reshape(N_EXPERTS))
    br = br.at[0, GROUP_LANE0:GROUP_LANE0 + N_GROUPS].set(grp_b)
    return pl.pallas_call(
        _router_kernel,
        grid=(t // tm,),
        in_specs=[
            pl.BlockSpec((tm, d), lambda i: (i, 0)),
            pl.BlockSpec((1, d), lambda i: (0, 0)),
            pl.BlockSpec((d, LANES), lambda i: (0, 0)),
            pl.BlockSpec((1, LANES), lambda i: (0, 0)),
        ],
        out_specs=[
            pl.BlockSpec((8, tm), lambda i: (0, i)),
            pl.BlockSpec((tm, LANES), lambda i: (i, 0)),
            pl.BlockSpec((1, LANES), lambda i: (0, 0)),
        ],
        out_shape=[
            jax.ShapeDtypeStruct((8, t), F32),
            jax.ShapeDtypeStruct((t, LANES), F32),
            jax.ShapeDtypeStruct((1, LANES), F32),
        ],
        scratch_shapes=[pltpu.VMEM((1, LANES), F32)],
        compiler_params=pltpu.CompilerParams(
            dimension_semantics=("arbitrary",), vmem_limit_bytes=VMEM_LIMIT),
        name="moe_router",
    )(hf, nw.reshape(1, d), wr, br)


def _store_row_tiles(ref, x):
    n = x.shape[0]
    for j in range(SUBLANES):
        ref[pl.ds(j, n, stride=SUBLANES), :] = x[:, j * LANES:(j + 1) * LANES]


def _load_row_tiles(ref):
    n = ref.shape[0] // SUBLANES
    return jnp.concatenate([ref[pl.ds(j, n, stride=SUBLANES), :] for j in range(SUBLANES)], axis=1)


def _row_tile(ref, r):
    return ref.at[pl.ds(pl.multiple_of(r * SUBLANES, SUBLANES), SUBLANES)]


def _dispatch_kernel(d0_ref, d1_ref, zf_ref, h_ref, nw_ref, xs_hbm, stage, zbuf, sem, zsem):
    i = pl.program_id(0)
    nb = pl.num_programs(0)
    tm = h_ref.shape[0]
    blk = zbuf.shape[0]
    slot = i % 2

    def wait_slot(s):
        for _ in range(2):
            pltpu.make_async_copy(xs_hbm.at[pl.ds(0, tm)], xs_hbm.at[pl.ds(0, tm)], sem.at[s]).wait()

    @pl.when(i == 0)
    def _():
        zbuf[...] = jnp.zeros_like(zbuf)

        def zero_block(start):
            return pltpu.make_async_copy(zbuf, xs_hbm.at[pl.ds(start, blk)], zsem)

        def zbody(e, started):
            @pl.when(zf_ref[e] >= 0)
            def _():
                zero_block(zf_ref[e]).start()
            return started + jnp.where(zf_ref[e] >= 0, 1, 0)
        started = lax.fori_loop(0, N_EXPERTS, zbody, 0)

        def tbody(b, carry):
            zero_block(b * blk).start()
            return carry
        n_blocks = xs_hbm.shape[0] // blk
        lax.fori_loop(zf_ref[N_EXPERTS], n_blocks, tbody, 0)

        def wbody(_, carry):
            zero_block(0).wait()
            return carry
        lax.fori_loop(0, started + n_blocks - zf_ref[N_EXPERTS], wbody, 0)

    @pl.when(i >= 2)
    def _():
        wait_slot(slot)

    _store_row_tiles(stage.at[slot], _rms(h_ref[...], nw_ref[...]))

    def body(rr, carry):
        for j in range(DMA_UNROLL):
            r = rr * DMA_UNROLL + j
            for kk, d_ref in enumerate((d0_ref, d1_ref)):
                pltpu.make_async_copy(_row_tile(stage.at[slot], r), xs_hbm.at[d_ref[i * tm + r]],
                                      sem.at[slot]).start(priority=kk)
        return carry
    lax.fori_loop(0, tm // DMA_UNROLL, body, 0)

    @pl.when(i == nb - 1)
    def _():
        wait_slot(slot)

        @pl.when(nb >= 2)
        def _():
            wait_slot(1 - slot)


def _dispatch(hf, nw, dest0, dest1, zfill, n_rows):
    t, d = hf.shape
    tm = DISPATCH_ROWS
    grid_spec = pltpu.PrefetchScalarGridSpec(
        num_scalar_prefetch=3,
        grid=(t // tm,),
        in_specs=[
            pl.BlockSpec((tm, d), lambda i, a, b, c: (i, 0)),
            pl.BlockSpec((1, d), lambda i, a, b, c: (0, 0)),
        ],
        out_specs=pl.BlockSpec(memory_space=pl.ANY),
        scratch_shapes=[pltpu.VMEM((2, tm * SUBLANES, d // SUBLANES), F32),
                        pltpu.VMEM((MOE_BLOCK, SUBLANES, d // SUBLANES), F32),
                        pltpu.SemaphoreType.DMA((2,)), pltpu.SemaphoreType.DMA(())],
    )
    return pl.pallas_call(
        _dispatch_kernel,
        grid_spec=grid_spec,
        out_shape=jax.ShapeDtypeStruct((n_rows, SUBLANES, d // SUBLANES), F32),
        compiler_params=pltpu.CompilerParams(
            dimension_semantics=("arbitrary",), vmem_limit_bytes=VMEM_LIMIT),
        name="moe_dispatch",
    )(dest0, dest1, zfill, hf, nw.reshape(1, d))


def _expert_kernel(be_ref, nu_ref, x_ref, wg_ref, wu_ref, wd_ref, y_ref, wg_s, wu_s, wd_s):
    i = pl.program_id(0)
    prev = be_ref[jnp.maximum(i - 1, 0)]

    @pl.when(i < nu_ref[0])
    def _():
        @pl.when((i == 0) | (be_ref[i] != prev))
        def _():
            wg_s[...] = wg_ref[0, 0].astype(BF16)
            wu_s[...] = wu_ref[0, 0].astype(BF16)
            wd_s[...] = wd_ref[0, 0].astype(BF16)

        hn = _load_row_tiles(x_ref).astype(BF16)
        g = jnp.dot(hn, wg_s[...], preferred_element_type=F32)
        u = jnp.dot(hn, wu_s[...], preferred_element_type=F32)
        a = (g * _sigmoid(g) * u).astype(BF16)
        _store_row_tiles(y_ref, jnp.dot(a, wd_s[...], preferred_element_type=F32))

    @pl.when(i >= nu_ref[0])
    def _():
        y_ref[...] = jnp.zeros_like(y_ref)


def _experts(xs, block_expert, n_used, w_gate, w_up, w_down, layer):
    n_rows = xs.shape[0]
    blk = MOE_BLOCK
    d, de = w_gate.shape[-2:]
    row_tile = (blk * SUBLANES, d // SUBLANES)
    used_blk = lambda i, be, nu: (jnp.minimum(i, nu[0] - 1), 0)
    grid_spec = pltpu.PrefetchScalarGridSpec(
        num_scalar_prefetch=2,
        grid=(n_rows // blk,),
        in_specs=[
            pl.BlockSpec(row_tile, used_blk),
            pl.BlockSpec((1, 1, d, de), lambda i, be, nu: (layer, be[i], 0, 0)),
            pl.BlockSpec((1, 1, d, de), lambda i, be, nu: (layer, be[i], 0, 0)),
            pl.BlockSpec((1, 1, de, d), lambda i, be, nu: (layer, be[i], 0, 0)),
        ],
        out_specs=pl.BlockSpec(row_tile, lambda i, be, nu: (i, 0)),
        scratch_shapes=[pltpu.VMEM((d, de), BF16), pltpu.VMEM((d, de), BF16),
                        pltpu.VMEM((de, d), BF16)],
    )
    return pl.pallas_call(
        _expert_kernel,
        grid_spec=grid_spec,
        out_shape=jax.ShapeDtypeStruct((n_rows * SUBLANES, d // SUBLANES), F32),
        compiler_params=pltpu.CompilerParams(
            dimension_semantics=("arbitrary",), vmem_limit_bytes=VMEM_LIMIT),
        name="moe_experts",
    )(block_expert, n_used, xs.reshape(n_rows * SUBLANES, d // SUBLANES), w_gate, w_up, w_down)


def _combine_kernel(d0_ref, d1_ref, y_hbm, h_ref, gate_ref, fw_ref, o_ref, ybuf, sem, *, final_norm):
    i = pl.program_id(0)
    nb = pl.num_programs(0)
    tc = h_ref.shape[0]

    def issue(b, slot):
        def body(rr, carry):
            for j in range(DMA_UNROLL):
                r = rr * DMA_UNROLL + j
                for kk, d_ref in enumerate((d0_ref, d1_ref)):
                    pltpu.make_async_copy(y_hbm.at[d_ref[b * tc + r]], _row_tile(ybuf.at[slot, kk], r),
                                          sem.at[slot]).start(priority=kk)
            return carry
        lax.fori_loop(0, tc // DMA_UNROLL, body, 0)

    @pl.when(i == 0)
    def _():
        issue(0, 0)

    @pl.when(i + 1 < nb)
    def _():
        issue(i + 1, (i + 1) % 2)

    slot = i % 2
    for kk in range(2):
        pltpu.make_async_copy(y_hbm.at[pl.ds(0, tc)], y_hbm.at[pl.ds(0, tc)], sem.at[slot]).wait()

    gate = gate_ref[...]
    y0 = _load_row_tiles(ybuf.at[slot, 0])
    y1 = _load_row_tiles(ybuf.at[slot, 1])
    o = h_ref[...] + (gate[:, 0:1] * y0 + gate[:, 1:2] * y1)
    if final_norm:
        o = _rms(o, fw_ref[...])
    o_ref[...] = o


def _combine(hf, y, dest0, dest1, gates, fw, final_norm):
    t, d = hf.shape
    tc = COMBINE_ROWS
    grid_spec = pltpu.PrefetchScalarGridSpec(
        num_scalar_prefetch=2,
        grid=(t // tc,),
        in_specs=[
            pl.BlockSpec(memory_space=pl.ANY),
            pl.BlockSpec((tc, d), lambda i, a, b: (i, 0)),
            pl.BlockSpec((tc, LANES), lambda i, a, b: (i, 0)),
            pl.BlockSpec((1, d), lambda i, a, b: (0, 0)),
        ],
        out_specs=pl.BlockSpec((tc, d), lambda i, a, b: (i, 0)),
        scratch_shapes=[pltpu.VMEM((2, 2, tc * SUBLANES, d // SUBLANES), F32),
                        pltpu.SemaphoreType.DMA((2,))],
    )
    return pl.pallas_call(
        functools.partial(_combine_kernel, final_norm=final_norm),
        grid_spec=grid_spec,
        out_shape=jax.ShapeDtypeStruct((t, d), F32),
        compiler_params=pltpu.CompilerParams(
            dimension_semantics=("arbitrary",), vmem_limit_bytes=VMEM_LIMIT),
        name="moe_combine",
    )(dest0, dest1, y.reshape(-1, SUBLANES, d // SUBLANES), hf, gates, fw.reshape(1, d))


def _moe_layer(h, nw, grp_w, grp_b, exp_w, exp_b, w_gate, w_up, w_down, fw, layer, final_norm):
    b, s, d = h.shape
    t = b * s
    hf = h.reshape(t, d)
    rec, gates, cnt = _router(hf, nw, grp_w, grp_b, exp_w, exp_b)

    rec = rec.astype(jnp.int32)
    counts = cnt[0, :N_EXPERTS].astype(jnp.int32)
    padded = (counts + MOE_BLOCK - 1) // MOE_BLOCK * MOE_BLOCK
    pad_end = jnp.cumsum(padded)
    pad_start = pad_end - padded
    eids = jnp.arange(N_EXPERTS, dtype=jnp.int32)[:, None]
    dest0 = jnp.sum(jnp.where(rec[0][None, :] == eids, pad_start[:, None], 0), axis=0) + rec[2]
    dest1 = jnp.sum(jnp.where(rec[1][None, :] == eids, pad_start[:, None], 0), axis=0) + rec[3]
    n_rows = 2 * t + N_EXPERTS * MOE_BLOCK
    n_blocks = n_rows // MOE_BLOCK
    block_start = jnp.arange(n_blocks, dtype=jnp.int32) * MOE_BLOCK
    block_expert = jnp.minimum(
        jnp.sum((block_start[:, None] >= pad_end[None, :]).astype(jnp.int32), axis=1), N_EXPERTS - 1)
    n_used = (pad_end[-1:] // MOE_BLOCK).astype(jnp.int32)
    zfill = jnp.concatenate([jnp.where(padded > 0, pad_end - MOE_BLOCK, -1).astype(jnp.int32), n_used])

    xs = _dispatch(hf, nw, dest0, dest1, zfill, n_rows)
    y = _experts(xs, block_expert, n_used, w_gate, w_up, w_down, layer)
    out = _combine(hf, y, dest0, dest1, gates, fw, final_norm)
    return out.reshape(b, s, d)


def kernel(x, conv_norm_w, conv_pw1_w, conv_pw1_b, conv_dw_w, conv_dw_b, conv_ln_g, conv_ln_b,
           conv_pw2_w, conv_pw2_b, hgrn_norm_w, hgrn_w_in, hgrn_gnorm_w, hgrn_w_out, lower_bounds,
           ffn_norm_w, router_grp_w, router_grp_b, router_exp_w, router_exp_b,
           moe_w_gate, moe_w_up, moe_w_down, final_norm_w):
    depth = lower_bounds.shape[0]
    lb_p = jax.nn.softmax(lower_bounds.astype(F32), axis=0)
    lb_all = jnp.cumsum(lb_p, axis=0) - lb_p[0]
    h = x
    for layer in range(depth):
        j = layer // 2
        if layer % 2 == 0:
            h = _conformer_layer(h, conv_norm_w[j], conv_pw1_w[j], conv_pw1_b[j], conv_dw_w[j],
                                 conv_dw_b[j], conv_ln_g[j], conv_ln_b[j], conv_pw2_w[j],
                                 conv_pw2_b[j])
        else:
            h = _hgrn_layer(h, hgrn_norm_w[j], hgrn_w_in[j], hgrn_gnorm_w[j], hgrn_w_out[j],
                            lb_all[layer])
        h = _moe_layer(h, ffn_norm_w[layer], router_grp_w[layer], router_grp_b[layer],
                       router_exp_w[layer], router_exp_b[layer], moe_w_gate, moe_w_up,
                       moe_w_down, final_norm_w, layer=layer, final_norm=(layer == depth - 1))
    return h
```

```python
import functools

import jax
import jax.numpy as jnp
from jax import lax
from jax.experimental import pallas as pl
from jax.experimental.pallas import tpu as pltpu

F32 = jnp.float32
BF16 = jnp.bfloat16

NORM_EPS = 1e-6
CONV_WIDTH = 31
CONV_HALO = 32
HGRN_HEADS = 8
HEAD_DIM = 128
CHUNK = 64
N_GROUPS = 4
EXPERTS_PER_GROUP = 8
N_EXPERTS = N_GROUPS * EXPERTS_PER_GROUP
LANES = 128
SUBLANES = 8
GROUP_LANE0 = N_EXPERTS

CONV_ROWS = 512
CONV_RC = 128
HGRN_ROWS = 1024
ROUTER_ROWS = 1024
RANK_ROWS = 256
MOE_BLOCK = 512
DMA_UNROLL = 8
DISPATCH_ROWS = 512
COMBINE_ROWS = 512
VMEM_LIMIT = 48 * 1024 * 1024


def _rms(x, w):
    ms = jnp.mean(x * x, axis=-1, keepdims=True)
    return x * lax.rsqrt(ms + NORM_EPS) * w


def _sigmoid(x):
    return 1.0 / (1.0 + jnp.exp(-x))


def _conv_kernel(x_ref, nw_ref, pw1_ref, b1_ref, dw_ref, dwb_ref, lng_ref, lnb_ref,
                 pw2_ref, b2_ref, o_ref, ubuf, cbuf):
    s = pl.program_id(1)
    ts = x_ref.shape[1]
    d = x_ref.shape[2]
    x = x_ref[0]
    hn = _rms(x, nw_ref[...]).astype(BF16)
    a = jnp.dot(hn, pw1_ref[...], preferred_element_type=F32) + b1_ref[...]
    u = a[:, :d] * _sigmoid(a[:, d:])

    @pl.when(s == 0)
    def _():
        ubuf[0:CONV_HALO, :] = jnp.zeros((CONV_HALO, d), F32)

    @pl.when(s > 0)
    def _():
        ubuf[0:CONV_HALO, :] = ubuf[ts:ts + CONV_HALO, :]

    ubuf[CONV_HALO:, :] = u

    off0 = CONV_HALO - (CONV_WIDTH - 1)
    win = CONV_RC + CONV_HALO
    for rc in range(ts // CONV_RC):
        for lc in range(d // LANES):
            ls = slice(lc * LANES, (lc + 1) * LANES)
            xw = ubuf[rc * CONV_RC:rc * CONV_RC + win, ls]
            acc = jnp.broadcast_to(dwb_ref[:, ls], (CONV_RC, LANES))
            for p in range(SUBLANES):
                xp = xw if p == 0 else pltpu.roll(xw, win - p, axis=0)
                for j in range(CONV_WIDTH):
                    if (off0 + j) % SUBLANES == p:
                        a0 = off0 + j - p
                        acc = acc + dw_ref[j:j + 1, ls] * xp[a0:a0 + CONV_RC, :]
            cbuf[rc * CONV_RC:(rc + 1) * CONV_RC, ls] = acc

    c = cbuf[...]
    mu = jnp.mean(c, axis=-1, keepdims=True)
    cc = c - mu
    var = jnp.mean(cc * cc, axis=-1, keepdims=True)
    n = cc * lax.rsqrt(var + NORM_EPS) * lng_ref[...] + lnb_ref[...]
    sw = (n * _sigmoid(n)).astype(BF16)
    y = jnp.dot(sw, pw2_ref[...], preferred_element_type=F32) + b2_ref[...]
    o_ref[0] = x + y


def _conformer_layer(h, nw, pw1, b1, dw, dwb, lng, lnb, pw2, b2):
    b, s, d = h.shape
    ts = CONV_ROWS
    row = lambda v: v.reshape(1, -1)
    full = lambda shape: pl.BlockSpec(shape, lambda bi, si: (0,) * len(shape))
    return pl.pallas_call(
        _conv_kernel,
        grid=(b, s // ts),
        in_specs=[
            pl.BlockSpec((1, ts, d), lambda bi, si: (bi, si, 0)),
            full((1, d)), full((d, 2 * d)), full((1, 2 * d)),
            full((CONV_WIDTH, d)), full((1, d)), full((1, d)), full((1, d)),
            full((d, d)), full((1, d)),
        ],
        out_specs=pl.BlockSpec((1, ts, d), lambda bi, si: (bi, si, 0)),
        out_shape=jax.ShapeDtypeStruct((b, s, d), F32),
        scratch_shapes=[pltpu.VMEM((ts + CONV_HALO, d), F32), pltpu.VMEM((ts, d), F32)],
        compiler_params=pltpu.CompilerParams(
            dimension_semantics=("arbitrary", "arbitrary"), vmem_limit_bytes=VMEM_LIMIT),
        name="conformer_conv",
    )(h, row(nw), pw1.astype(BF16), row(b1), dw, row(dwb), row(lng), row(lnb),
      pw2.astype(BF16), row(b2))


HGRN_GROUP = 4
GROUP_ROWS = HGRN_GROUP * CHUNK


def _hgrn_state_updates(gi, k_s, v_s, g_s, upd_s):
    rows = pl.ds(pl.multiple_of(gi * GROUP_ROWS, GROUP_ROWS), GROUP_ROWS)
    gc = g_s[rows, :]
    k = k_s[rows, :]
    chunk_of_row = lax.broadcasted_iota(jnp.int32, (GROUP_ROWS, HEAD_DIM), 0) // CHUNK
    glast = jnp.concatenate(
        [jnp.broadcast_to(gc[(j + 1) * CHUNK - 1:(j + 1) * CHUNK, :], (CHUNK, HEAD_DIM))
         for j in range(HGRN_GROUP)], axis=0)
    kd = k * jnp.exp(glast - gc)
    rhs = jnp.concatenate([jnp.where(chunk_of_row == j, kd, 0.0).astype(BF16)
                           for j in range(HGRN_GROUP)], axis=1)
    vb = v_s[rows, :].astype(BF16)
    upd = lax.dot_general(vb, rhs, (((0,), (0,)), ((), ())), preferred_element_type=F32)
    for j in range(HGRN_GROUP):
        upd_s[gi * HGRN_GROUP + j] = upd[:, j * HEAD_DIM:(j + 1) * HEAD_DIM]


def _hgrn_group(gi, q_s, k_s, v_s, g_s, o_s, sb_s):
    n = GROUP_ROWS
    rows = pl.ds(pl.multiple_of(gi * n, n), n)
    q = q_s[rows, :]
    k = k_s[rows, :]
    gc = g_s[rows, :]
    vb = v_s[rows, :].astype(BF16)
    nt = (((1,), (1,)), ((), ()))

    qg = (q * jnp.exp(gc)).astype(BF16)
    sts = sb_s[pl.ds(pl.multiple_of(gi * HGRN_GROUP * HEAD_DIM, HGRN_GROUP * HEAD_DIM),
                     HGRN_GROUP * HEAD_DIM), :]
    wide = lax.dot_general(qg, sts, nt, preferred_element_type=F32)
    inter = jnp.concatenate(
        [wide[j * CHUNK:(j + 1) * CHUNK, j * HEAD_DIM:(j + 1) * HEAD_DIM]
         for j in range(HGRN_GROUP)], axis=0)

    row = lax.broadcasted_iota(jnp.int32, (n, HEAD_DIM), 0)
    ti = lax.broadcasted_iota(jnp.int32, (n, n), 0)
    si = lax.broadcasted_iota(jnp.int32, (n, n), 1)
    tx = ti ^ si
    scores = jnp.broadcast_to(jnp.sum(q * k, axis=-1, keepdims=True), (n, n))
    end = gc
    b = 1
    while b < CHUNK:
        upper = (row & b) != 0
        e = jnp.exp(jnp.where(upper, gc - pltpu.roll(end, b, axis=0), end - gc))
        m = (jnp.where(upper, q, k) * e).astype(BF16)
        sb = lax.dot_general(m, m, nt, preferred_element_type=F32)
        scores = jnp.where(tx >= b, sb, scores)
        end = jnp.where(upper, end, pltpu.roll(end, n - b, axis=0))
        b *= 2
    scores = jnp.where((ti >= si) & (tx < CHUNK), scores, 0.0)

    intra = jnp.dot(scores.astype(BF16), vb, preferred_element_type=F32)
    o_s[rows, :] = inter + intra


def _hgrn_kernel(x_ref, nw_ref, win_ref, lb_ref, gn_ref, wout_ref, o_ref,
                 hn_s, proj_s, q_s, k_s, v_s, g_s, z_s, o_s, oh_s, upd_s, sb_s, st_ref):
    s = pl.program_id(1)
    h = pl.program_id(2)
    nh = pl.num_programs(2)
    ts = x_ref.shape[1]

    @pl.when(h == 0)
    def _():
        hn_s[...] = _rms(x_ref[0], nw_ref[...]).astype(BF16)
        proj_s[...] = jnp.dot(hn_s[...], win_ref[0], preferred_element_type=F32)

    @pl.when(s == 0)
    def _():
        st_ref[h] = jnp.zeros((HEAD_DIM, HEAD_DIM), F32)

    proj = proj_s[...]
    qr = proj[:, 0:HEAD_DIM]
    q_s[...] = qr * _sigmoid(qr)
    lb = lb_ref[0]
    forget = lb + (1.0 - lb) * _sigmoid(proj[:, HEAD_DIM:2 * HEAD_DIM])
    k_s[...] = 1.0 - forget
    v_s[...] = proj[:, 2 * HEAD_DIM:3 * HEAD_DIM]
    zr = proj[:, 3 * HEAD_DIM:4 * HEAD_DIM]
    z_s[...] = zr * _sigmoid(zr)

    g = jnp.log(forget)
    rmod = lax.broadcasted_iota(jnp.int32, (ts, 1), 0) % CHUNK
    dstep = 1
    while dstep < CHUNK:
        g = g + jnp.where(rmod >= dstep, pltpu.roll(g, dstep, axis=0), 0.0)
        dstep *= 2
    g_s[...] = g

    proj_s[...] = jnp.dot(hn_s[...], win_ref[jnp.minimum(h + 1, nh - 1)],
                          preferred_element_type=F32)

    nchunk = ts // CHUNK
    ngroup = ts // GROUP_ROWS
    for gi in range(ngroup):
        _hgrn_state_updates(gi, k_s, v_s, g_s, upd_s)

    st = st_ref[h]
    for c in range(nchunk):
        sb_s[c * HEAD_DIM:(c + 1) * HEAD_DIM, :] = st.astype(BF16)
        glast = g_s[(c + 1) * CHUNK - 1:(c + 1) * CHUNK, :]
        st = jnp.exp(glast) * st + upd_s[c]
    st_ref[h] = st

    for gi in range(ngroup):
        _hgrn_group(gi, q_s, k_s, v_s, g_s, o_s, sb_s)

    o = o_s[...]
    o = o * lax.rsqrt(jnp.mean(o * o, axis=-1, keepdims=True) + NORM_EPS) * gn_ref[...]
    oh_s[h] = (o * z_s[...]).astype(BF16)

    @pl.when(h == pl.num_programs(2) - 1)
    def _():
        oall = jnp.concatenate([oh_s[hh] for hh in range(HGRN_HEADS)], axis=1)
        o_ref[0] = x_ref[0] + jnp.dot(oall, wout_ref[...], preferred_element_type=F32)


def _hgrn_layer(h, nw, w_in, gn, w_out, lb):
    b, s, d = h.shape
    ts = HGRN_ROWS
    nh = HGRN_HEADS
    win_r = w_in.astype(BF16).reshape(d, 4, nh, HEAD_DIM).transpose(2, 0, 1, 3).reshape(
        nh, d, 4 * HEAD_DIM)
    return pl.pallas_call(
        _hgrn_kernel,
        grid=(b, s // ts, nh),
        in_specs=[
            pl.BlockSpec((1, ts, d), lambda bi, si, hi: (bi, si, 0)),
            pl.BlockSpec((1, d), lambda bi, si, hi: (0, 0)),
            pl.BlockSpec((nh, d, 4 * HEAD_DIM), lambda bi, si, hi: (0, 0, 0),
                         pipeline_mode=pl.Buffered(1)),
            pl.BlockSpec((1, 1, HEAD_DIM), lambda bi, si, hi: (hi, 0, 0)),
            pl.BlockSpec((1, HEAD_DIM), lambda bi, si, hi: (0, 0)),
            pl.BlockSpec((d, d), lambda bi, si, hi: (0, 0), pipeline_mode=pl.Buffered(1)),
        ],
        out_specs=pl.BlockSpec((1, ts, d), lambda bi, si, hi: (bi, si, 0)),
        out_shape=jax.ShapeDtypeStruct((b, s, d), F32),
        scratch_shapes=[
            pltpu.VMEM((ts, d), BF16),
            pltpu.VMEM((ts, 4 * HEAD_DIM), F32),
            pltpu.VMEM((ts, HEAD_DIM), F32), pltpu.VMEM((ts, HEAD_DIM), F32),
            pltpu.VMEM((ts, HEAD_DIM), F32), pltpu.VMEM((ts, HEAD_DIM), F32),
            pltpu.VMEM((ts, HEAD_DIM), F32), pltpu.VMEM((ts, HEAD_DIM), F32),
            pltpu.VMEM((nh, ts, HEAD_DIM), BF16),
            pltpu.VMEM((ts // CHUNK, HEAD_DIM, HEAD_DIM), F32),
            pltpu.VMEM((ts // CHUNK * HEAD_DIM, HEAD_DIM), BF16),
            pltpu.VMEM((nh, HEAD_DIM, HEAD_DIM), F32),
        ],
        compiler_params=pltpu.CompilerParams(
            dimension_semantics=("arbitrary", "arbitrary", "arbitrary"),
            vmem_limit_bytes=VMEM_LIMIT),
        name="hgrn2",
    )(h, nw.reshape(1, d), win_r, lb.reshape(nh, 1, HEAD_DIM),
      gn.reshape(1, HEAD_DIM), w_out.astype(BF16))


def _router_kernel(h_ref, nw_ref, wr_ref, br_ref, idx_ref, gate_ref, cnt_ref, cnt_s):
    i = pl.program_id(0)
    tm = h_ref.shape[0]

    @pl.when(i == 0)
    def _():
        cnt_s[...] = jnp.zeros_like(cnt_s)

    hn = _rms(h_ref[...], nw_ref[...])
    h1 = hn.astype(BF16)
    h2 = (hn - h1.astype(F32)).astype(BF16)
    w = wr_ref[...]
    w1 = w.astype(BF16)
    w2 = (w - w1.astype(F32)).astype(BF16)
    logits = (jnp.dot(h1, w1, preferred_element_type=F32) + jnp.dot(h1, w2, preferred_element_type=F32)
              + jnp.dot(h2, w1, preferred_element_type=F32)) + br_ref[...]
    li = lax.broadcasted_iota(jnp.int32, (tm, LANES), 1).astype(F32)
    neg = jnp.float32(-jnp.inf)
    big = jnp.float32(LANES)

    gl = jnp.where((li >= GROUP_LANE0) & (li < GROUP_LANE0 + N_GROUPS), logits, neg)
    gmax = jnp.max(gl, axis=-1, keepdims=True)
    gval = 1.0 / jnp.sum(jnp.exp(gl - gmax), axis=-1, keepdims=True)
    gidx = jnp.min(jnp.where(gl == gmax, li, big), axis=-1, keepdims=True) - GROUP_LANE0

    lo = gidx * EXPERTS_PER_GROUP
    el = jnp.where((li >= lo) & (li < lo + EXPERTS_PER_GROUP), logits, neg)
    v0 = jnp.max(el, axis=-1, keepdims=True)
    i0 = jnp.min(jnp.where(el == v0, li, big), axis=-1, keepdims=True)
    el2 = jnp.where(li == i0, neg, el)
    v1 = jnp.max(el2, axis=-1, keepdims=True)
    i1 = jnp.min(jnp.where(el2 == v1, li, big), axis=-1, keepdims=True)
    t = jnp.exp(v1 - v0)
    p0 = 1.0 / (1.0 + t)
    g0 = gval * p0
    g1 = gval * (t * p0)

    oh = jnp.where((li == i0) | (li == i1), 1.0, 0.0)
    rr = lax.broadcasted_iota(jnp.int32, (RANK_ROWS, RANK_ROWS), 0)
    cc = lax.broadcasted_iota(jnp.int32, (RANK_ROWS, RANK_ROWS), 1)
    lower = jnp.where(rr > cc, 1.0, 0.0).astype(BF16)
    ohb = oh.astype(BF16)
    carry = cnt_s[...]
    parts = []
    for sb in range(tm // RANK_ROWS):
        blk_oh = ohb[sb * RANK_ROWS:(sb + 1) * RANK_ROWS, :]
        parts.append(jnp.dot(lower, blk_oh, preferred_element_type=F32) + carry)
        carry = carry + jnp.sum(oh[sb * RANK_ROWS:(sb + 1) * RANK_ROWS, :], axis=0, keepdims=True)
    before = jnp.concatenate(parts, axis=0)
    r0 = jnp.sum(jnp.where(li == i0, before, 0.0), axis=-1, keepdims=True)
    r1 = jnp.sum(jnp.where(li == i1, before, 0.0), axis=-1, keepdims=True)
    cnt_s[...] = carry

    rec = jnp.where(li == 0.0, i0, jnp.where(li == 1.0, i1,
                    jnp.where(li == 2.0, r0, jnp.where(li == 3.0, r1, 0.0))))
    idx_ref[...] = rec.T[0:8, :]
    gate_ref[...] = jnp.where(li == 0.0, g0, jnp.where(li == 1.0, g1, 0.0))
    cnt_ref[...] = cnt_s[...]


def _router(hf, nw, grp_w, grp_b, exp_w, exp_b):
    t, d = hf.shape
    tm = ROUTER_ROWS
    wr = jnp.zeros((d, LANES), F32)
    wr = wr.at[:, :N_EXPERTS].set(exp_w.reshape(d, N_EXPERTS))
    wr = wr.at[:, GROUP_LANE0:GROUP_LANE0 + N_GROUPS].set(grp_w)
    br = jnp.zeros((1, LANES), F32)
    br = br.at[0, :N_EXPERTS].set(exp_b.reshape(N_EXPERTS))
    br = br.at[0, GROUP_LANE0:GROUP_LANE0 + N_GROUPS].set(grp_b)
    return pl.pallas_call(
        _router_kernel,
        grid=(t // tm,),
        in_specs=[
            pl.BlockSpec((tm, d), lambda i: (i, 0)),
            pl.BlockSpec((1, d), lambda i: (0, 0)),
            pl.BlockSpec((d, LANES), lambda i: (0, 0)),
            pl.BlockSpec((1, LANES), lambda i: (0, 0)),
        ],
        out_specs=[
            pl.BlockSpec((8, tm), lambda i: (0, i)),
            pl.BlockSpec((tm, LANES), lambda i: (i, 0)),
            pl.BlockSpec((1, LANES), lambda i: (0, 0)),
        ],
        out_shape=[
            jax.ShapeDtypeStruct((8, t), F32),
            jax.ShapeDtypeStruct((t, LANES), F32),
            jax.ShapeDtypeStruct((1, LANES), F32),
        ],
        scratch_shapes=[pltpu.VMEM((1, LANES), F32)],
        compiler_params=pltpu.CompilerParams(
            dimension_semantics=("arbitrary",), vmem_limit_bytes=VMEM_LIMIT),
        name="moe_router",
    )(hf, nw.reshape(1, d), wr, br)


def _store_row_tiles(ref, x):
    n = x.shape[0]
    for j in range(SUBLANES):
        ref[pl.ds(j, n, stride=SUBLANES), :] = x[:, j * LANES:(j + 1) * LANES]


def _load_row_tiles(ref):
    n = ref.shape[0] // SUBLANES
    return jnp.concatenate([ref[pl.ds(j, n, stride=SUBLANES), :] for j in range(SUBLANES)], axis=1)


def _row_tile(ref, r):
    return ref.at[pl.ds(pl.multiple_of(r * SUBLANES, SUBLANES), SUBLANES)]


def _dispatch_kernel(d0_ref, d1_ref, zf_ref, h_ref, nw_ref, xs_hbm, stage, zbuf, sem, zsem):
    i = pl.program_id(0)
    nb = pl.num_programs(0)
    tm = h_ref.shape[0]
    blk = zbuf.shape[0]
    slot = i % 2

    def wait_slot(s):
        for _ in range(2):
            pltpu.make_async_copy(xs_hbm.at[pl.ds(0, tm)], xs_hbm.at[pl.ds(0, tm)], sem.at[s]).wait()

    @pl.when(i == 0)
    def _():
        zbuf[...] = jnp.zeros_like(zbuf)

        def zero_block(start):
            return pltpu.make_async_copy(zbuf, xs_hbm.at[pl.ds(start, blk)], zsem)

        def zbody(e, started):
            @pl.when(zf_ref[e] >= 0)
            def _():
                zero_block(zf_ref[e]).start()
            return started + jnp.where(zf_ref[e] >= 0, 1, 0)
        started = lax.fori_loop(0, N_EXPERTS, zbody, 0)

        def tbody(b, carry):
            zero_block(b * blk).start()
            return carry
        n_blocks = xs_hbm.shape[0] // blk
        lax.fori_loop(zf_ref[N_EXPERTS], n_blocks, tbody, 0)

        def wbody(_, carry):
            zero_block(0).wait()
            return carry
        lax.fori_loop(0, started + n_blocks - zf_ref[N_EXPERTS], wbody, 0)

    @pl.when(i >= 2)
    def _():
        wait_slot(slot)

    _store_row_tiles(stage.at[slot], _rms(h_ref[...], nw_ref[...]))

    def body(rr, carry):
        for j in range(DMA_UNROLL):
            r = rr * DMA_UNROLL + j
            for kk, d_ref in enumerate((d0_ref, d1_ref)):
                pltpu.make_async_copy(_row_tile(stage.at[slot], r), xs_hbm.at[d_ref[i * tm + r]],
                                      sem.at[slot]).start(priority=kk)
        return carry
    lax.fori_loop(0, tm // DMA_UNROLL, body, 0)

    @pl.when(i == nb - 1)
    def _():
        wait_slot(slot)

        @pl.when(nb >= 2)
        def _():
            wait_slot(1 - slot)


def _dispatch(hf, nw, dest0, dest1, zfill, n_rows):
    t, d = hf.shape
    tm = DISPATCH_ROWS
    grid_spec = pltpu.PrefetchScalarGridSpec(
        num_scalar_prefetch=3,
        grid=(t // tm,),
        in_specs=[
            pl.BlockSpec((tm, d), lambda i, a, b, c: (i, 0)),
            pl.BlockSpec((1, d), lambda i, a, b, c: (0, 0)),
        ],
        out_specs=pl.BlockSpec(memory_space=pl.ANY),
        scratch_shapes=[pltpu.VMEM((2, tm * SUBLANES, d // SUBLANES), F32),
                        pltpu.VMEM((MOE_BLOCK, SUBLANES, d // SUBLANES), F32),
                        pltpu.SemaphoreType.DMA((2,)), pltpu.SemaphoreType.DMA(())],
    )
    return pl.pallas_call(
        _dispatch_kernel,
        grid_spec=grid_spec,
        out_shape=jax.ShapeDtypeStruct((n_rows, SUBLANES, d // SUBLANES), F32),
        compiler_params=pltpu.CompilerParams(
            dimension_semantics=("arbitrary",), vmem_limit_bytes=VMEM_LIMIT),
        name="moe_dispatch",
    )(dest0, dest1, zfill, hf, nw.reshape(1, d))


def _expert_kernel(be_ref, nu_ref, x_ref, wg_ref, wu_ref, wd_ref, y_ref, wg_s, wu_s, wd_s):
    i = pl.program_id(0)
    prev = be_ref[jnp.maximum(i - 1, 0)]

    @pl.when(i < nu_ref[0])
    def _():
        @pl.when((i == 0) | (be_ref[i] != prev))
        def _():
            wg_s[...] = wg_ref[0, 0].astype(BF16)
            wu_s[...] = wu_ref[0, 0].astype(BF16)
            wd_s[...] = wd_ref[0, 0].astype(BF16)

        hn = _load_row_tiles(x_ref).astype(BF16)
        g = jnp.dot(hn, wg_s[...], preferred_element_type=F32)
        u = jnp.dot(hn, wu_s[...], preferred_element_type=F32)
        a = (g * _sigmoid(g) * u).astype(BF16)
        _store_row_tiles(y_ref, jnp.dot(a, wd_s[...], preferred_element_type=F32))

    @pl.when(i >= nu_ref[0])
    def _():
        y_ref[...] = jnp.zeros_like(y_ref)


def _experts(xs, block_expert, n_used, w_gate, w_up, w_down, layer):
    n_rows = xs.shape[0]
    blk = MOE_BLOCK
    d, de = w_gate.shape[-2:]
    row_tile = (blk * SUBLANES, d // SUBLANES)
    used_blk = lambda i, be, nu: (jnp.minimum(i, nu[0] - 1), 0)
    grid_spec = pltpu.PrefetchScalarGridSpec(
        num_scalar_prefetch=2,
        grid=(n_rows // blk,),
        in_specs=[
            pl.BlockSpec(row_tile, used_blk),
            pl.BlockSpec((1, 1, d, de), lambda i, be, nu: (layer, be[i], 0, 0)),
            pl.BlockSpec((1, 1, d, de), lambda i, be, nu: (layer, be[i], 0, 0)),
            pl.BlockSpec((1, 1, de, d), lambda i, be, nu: (layer, be[i], 0, 0)),
        ],
        out_specs=pl.BlockSpec(row_tile, lambda i, be, nu: (i, 0)),
        scratch_shapes=[pltpu.VMEM((d, de), BF16), pltpu.VMEM((d, de), BF16),
                        pltpu.VMEM((de, d), BF16)],
    )
    return pl.pallas_call(
        _expert_kernel,
        grid_spec=grid_spec,
        out_shape=jax.ShapeDtypeStruct((n_rows * SUBLANES, d // SUBLANES), F32),
        compiler_params=pltpu.CompilerParams(
            dimension_semantics=("arbitrary",), vmem_limit_bytes=VMEM_LIMIT),
        name="moe_experts",
    )(block_expert, n_used, xs.reshape(n_rows * SUBLANES, d // SUBLANES), w_gate, w_up, w_down)


def _combine_kernel(d0_ref, d1_ref, y_hbm, h_ref, gate_ref, fw_ref, o_ref, ybuf, sem, *, final_norm):
    i = pl.program_id(0)
    nb = pl.num_programs(0)
    tc = h_ref.shape[0]

    def issue(b, slot):
        def body(rr, carry):
            for j in range(DMA_UNROLL):
                r = rr * DMA_UNROLL + j
                for kk, d_ref in enumerate((d0_ref, d1_ref)):
                    pltpu.make_async_copy(y_hbm.at[d_ref[b * tc + r]], _row_tile(ybuf.at[slot, kk], r),
                                          sem.at[slot]).start(priority=kk)
            return carry
        lax.fori_loop(0, tc // DMA_UNROLL, body, 0)

    @pl.when(i == 0)
    def _():
        issue(0, 0)

    @pl.when(i + 1 < nb)
    def _():
        issue(i + 1, (i + 1) % 2)

    slot = i % 2
    for kk in range(2):
        pltpu.make_async_copy(y_hbm.at[pl.ds(0, tc)], y_hbm.at[pl.ds(0, tc)], sem.at[slot]).wait()

    gate = gate_ref[...]
    y0 = _load_row_tiles(ybuf.at[slot, 0])
    y1 = _load_row_tiles(ybuf.at[slot, 1])
    o = h_ref[...] + (gate[:, 0:1] * y0 + gate[:, 1:2] * y1)
    if final_norm:
        o = _rms(o, fw_ref[...])
    o_ref[...] = o


def _combine(hf, y, dest0, dest1, gates, fw, final_norm):
    t, d = hf.shape
    tc = COMBINE_ROWS
    grid_spec = pltpu.PrefetchScalarGridSpec(
        num_scalar_prefetch=2,
        grid=(t // tc,),
        in_specs=[
            pl.BlockSpec(memory_space=pl.ANY),
            pl.BlockSpec((tc, d), lambda i, a, b: (i, 0)),
            pl.BlockSpec((tc, LANES), lambda i, a, b: (i, 0)),
            pl.BlockSpec((1, d), lambda i, a, b: (0, 0)),
        ],
        out_specs=pl.BlockSpec((tc, d), lambda i, a, b: (i, 0)),
        scratch_shapes=[pltpu.VMEM((2, 2, tc * SUBLANES, d // SUBLANES), F32),
                        pltpu.SemaphoreType.DMA((2,))],
    )
    return pl.pallas_call(
        functools.partial(_combine_kernel, final_norm=final_norm),
        grid_spec=grid_spec,
        out_shape=jax.ShapeDtypeStruct((t, d), F32),
        compiler_params=pltpu.CompilerParams(
            dimension_semantics=("arbitrary",), vmem_limit_bytes=VMEM_LIMIT),
        name="moe_combine",
    )(dest0, dest1, y.reshape(-1, SUBLANES, d // SUBLANES), hf, gates, fw.reshape(1, d))


def _moe_layer(h, nw, grp_w, grp_b, exp_w, exp_b, w_gate, w_up, w_down, fw, layer, final_norm):
    b, s, d = h.shape
    t = b * s
    hf = h.reshape(t, d)
    rec, gates, cnt = _router(hf, nw, grp_w, grp_b, exp_w, exp_b)

    rec = rec.astype(jnp.int32)
    counts = cnt[0, :N_EXPERTS].astype(jnp.int32)
    padded = (counts + MOE_BLOCK - 1) // MOE_BLOCK * MOE_BLOCK
    pad_end = jnp.cumsum(padded)
    pad_start = pad_end - padded
    eids = jnp.arange(N_EXPERTS, dtype=jnp.int32)[:, None]
    dest0 = jnp.sum(jnp.where(rec[0][None, :] == eids, pad_start[:, None], 0), axis=0) + rec[2]
    dest1 = jnp.sum(jnp.where(rec[1][None, :] == eids, pad_start[:, None], 0), axis=0) + rec[3]
    n_rows = 2 * t + N_EXPERTS * MOE_BLOCK
    n_blocks = n_rows // MOE_BLOCK
    block_start = jnp.arange(n_blocks, dtype=jnp.int32) * MOE_BLOCK
    block_expert = jnp.minimum(
        jnp.sum((block_start[:, None] >= pad_end[None, :]).astype(jnp.int32), axis=1), N_EXPERTS - 1)
    n_used = (pad_end[-1:] // MOE_BLOCK).astype(jnp.int32)
    zfill = jnp.concatenate([jnp.where(padded > 0, pad_end - MOE_BLOCK, -1).astype(jnp.int32), n_used])

    xs = _dispatch(hf, nw, dest0, dest1, zfill, n_rows)
    y = _experts(xs, block_expert, n_used, w_gate, w_up, w_down, layer)
    out = _combine(hf, y, dest0, dest1, gates, fw, final_norm)
    return out.reshape(b, s, d)


def kernel(x, conv_norm_w, conv_pw1_w, conv_pw1_b, conv_dw_w, conv_dw_b, conv_ln_g, conv_ln_b,
           conv_pw2_w, conv_pw2_b, hgrn_norm_w, hgrn_w_in, hgrn_gnorm_w, hgrn_w_out, lower_bounds,
           ffn_norm_w, router_grp_w, router_grp_b, router_exp_w, router_exp_b,
           moe_w_gate, moe_w_up, moe_w_down, final_norm_w):
    depth = lower_bounds.shape[0]
    lb_p = jax.nn.softmax(lower_bounds.astype(F32), axis=0)
    lb_all = jnp.cumsum(lb_p, axis=0) - lb_p[0]
    h = x
    for layer in range(depth):
        j = layer // 2
        if layer % 2 == 0:
            h = _conformer_layer(h, conv_norm_w[j], conv_pw1_w[j], conv_pw1_b[j], conv_dw_w[j],
                                 conv_dw_b[j], conv_ln_g[j], conv_ln_b[j], conv_pw2_w[j],
                                 conv_pw2_b[j])
        else:
            h = _hgrn_layer(h, hgrn_norm_w[j], hgrn_w_in[j], hgrn_gnorm_w[j], hgrn_w_out[j],
                            lb_all[layer])
        h = _moe_layer(h, ffn_norm_w[layer], router_grp_w[layer], router_grp_b[layer],
                       router_exp_w[layer], router_exp_b[layer], moe_w_gate, moe_w_up,
                       moe_w_down, final_norm_w, layer=layer, final_norm=(layer == depth - 1))
    return h
```
